```python
import math
import jax, jax.numpy as jnp
from jax import lax
import numpy as np

D_MODEL = 1024
BATCH = 32
SEQ = 256
DEPTH = 4
DEC_BATCH = 4
DEC_SEQ = 4096
PAST_LEN = 256

GRID_W = 64
MIX_WIDTH = D_MODEL
HEAD_DIM = 64
ATTN_HEADS = 4
KV_HEADS = 2
Q_PER_KV = ATTN_HEADS // KV_HEADS
ATTN_WIDTH = ATTN_HEADS * HEAD_DIM
WINDOW = 128
ATTN_BLOCK = 128
ROPE_THETA = 10000.0
SSD_HEAD_DIM = 64
SSD_WIDTH = MIX_WIDTH // 2
SSD_HEADS = SSD_WIDTH // SSD_HEAD_DIM
SSD_GROUPS = 2
SSD_STATE = 64
SSD_CONV = 5
SSD_CHUNK = 128
SSD_XBC = SSD_WIDTH + 2 * SSD_GROUPS * SSD_STATE
CONV_WIDTH = MIX_WIDTH - ATTN_WIDTH - SSD_WIDTH
CONV_KERNEL = 31
CONV_GROUPS = 4
IN_SIZES = (ATTN_WIDTH, KV_HEADS * HEAD_DIM, KV_HEADS * HEAD_DIM, SSD_WIDTH, SSD_XBC, SSD_HEADS, 2 * CONV_WIDTH)
IN_COLS = sum(IN_SIZES)
N_EXPERT_GROUPS = 4
EXPERTS_PER_GROUP = 8
N_EXPERTS = N_EXPERT_GROUPS * EXPERTS_PER_GROUP
TOP_K = 2
EXPERT_FF = 256
MOE_BLOCK = 128
EPS = 1e-6

kernel_name = 'hybrid_diffusion_prefix_trunk_step'


def rms_norm(x, w):
    xf = x.astype(jnp.float32)
    y = xf * lax.rsqrt(jnp.mean(xf * xf, axis=-1, keepdims=True) + EPS)
    return (y * w.astype(jnp.float32)).astype(x.dtype)


def group_layer_norm(x, w, b):
    shp = x.shape
    xg = x.reshape(*shp[:-1], CONV_GROUPS, shp[-1] // CONV_GROUPS).astype(jnp.float32)
    xc = xg - jnp.mean(xg, axis=-1, keepdims=True)
    y = (xc * lax.rsqrt(jnp.mean(xc * xc, axis=-1, keepdims=True) + EPS)).reshape(shp)
    return (y * w.astype(jnp.float32) + b.astype(jnp.float32)).astype(x.dtype)


def depthwise_conv(x, w, b):
    half = (w.shape[0] - 1) // 2
    y = lax.conv_general_dilated(x, w[:, None, :].astype(x.dtype), window_strides=(1,),
                                 padding=[(half, half)], dimension_numbers=('NWC', 'WIO', 'NWC'),
                                 feature_group_count=x.shape[-1])
    return y + b


def modulation(cond, w, b):
    m = jax.nn.silu(cond) @ w + b
    return jnp.split(m[..., None, :], 6, axis=-1)


def axial_rope(n_tokens):
    rows = n_tokens // GRID_W
    row = jnp.repeat(jnp.arange(rows, dtype=jnp.float32), GRID_W)
    col = jnp.tile(jnp.arange(GRID_W, dtype=jnp.float32), rows)
    n_freq = HEAD_DIM // 4
    inv = ROPE_THETA ** (-jnp.arange(n_freq, dtype=jnp.float32) / n_freq)
    ang = jnp.stack([row[:, None] * inv, col[:, None] * inv], axis=1)
    return jnp.cos(ang), jnp.sin(ang)


def apply_axial_rope(x, cos, sin):
    shp = x.shape
    xr = x.reshape(*shp[:-1], 2, 2, HEAD_DIM // 4)
    x1, x2 = xr[..., 0, :], xr[..., 1, :]
    c = cos[:, None].astype(x.dtype)
    s = sin[:, None].astype(x.dtype)
    return jnp.stack([x1 * c - x2 * s, x2 * c + x1 * s], axis=-2).reshape(shp)


def block_attention(q, k_ctx, v_ctx, sink, k_lat=None, v_lat=None):
    bsz, s_len = q.shape[0], q.shape[1]
    n_blk = s_len // ATTN_BLOCK
    c_len = k_ctx.shape[1]
    qb = q.reshape(bsz, n_blk, ATTN_BLOCK, KV_HEADS, Q_PER_KV, HEAD_DIM) * (HEAD_DIM ** -0.5)
    sink_logit = jnp.broadcast_to(sink.astype(jnp.float32).reshape(1, KV_HEADS, Q_PER_KV, 1, 1),
                                  (bsz, KV_HEADS, Q_PER_KV, ATTN_BLOCK, 1))
    local = k_lat is not None
    if local:
        zpad = jnp.zeros((bsz, ATTN_BLOCK, KV_HEADS, HEAD_DIM), k_lat.dtype)
        k_pad = jnp.concatenate([zpad, k_lat, zpad], axis=1)
        v_pad = jnp.concatenate([zpad.astype(v_lat.dtype), v_lat, zpad.astype(v_lat.dtype)], axis=1)
        q_off = jnp.arange(ATTN_BLOCK)
        k_off = jnp.arange(3 * ATTN_BLOCK) - ATTN_BLOCK
        band = jnp.abs(k_off[None, :] - q_off[:, None]) <= WINDOW

    def one_block(b):
        qblk = lax.dynamic_index_in_dim(qb, b, axis=1, keepdims=False)
        parts = [jnp.einsum('bqkgd,bckd->bkgqc', qblk, k_ctx).astype(jnp.float32)]
        if local:
            k_blk = lax.dynamic_slice_in_dim(k_pad, b * ATTN_BLOCK, 3 * ATTN_BLOCK, axis=1)
            v_blk = lax.dynamic_slice_in_dim(v_pad, b * ATTN_BLOCK, 3 * ATTN_BLOCK, axis=1)
            pos = b * ATTN_BLOCK + k_off
            valid = band & ((pos >= 0) & (pos < s_len))[None, :]
            s_loc = jnp.einsum('bqkgd,bskd->bkgqs', qblk, k_blk).astype(jnp.float32)
            parts.append(jnp.where(valid, s_loc, -jnp.inf))
        p = jax.nn.softmax(jnp.concatenate(parts + [sink_logit], axis=-1), axis=-1).astype(v_ctx.dtype)
        out = jnp.einsum('bkgqc,bckd->bqkgd', p[..., :c_len], v_ctx)
        if local:
            out = out + jnp.einsum('bkgqs,bskd->bqkgd', p[..., c_len:c_len + 3 * ATTN_BLOCK], v_blk)
        return out

    out = lax.map(one_block, jnp.arange(n_blk))
    return jnp.moveaxis(out, 0, 1).reshape(bsz, s_len, ATTN_WIDTH)


def ssd_scan(x, log_a, bm, cm, h0):
    bsz, l_len, n_h, p_dim = x.shape
    t = SSD_CHUNK
    nc = l_len // t
    xc = x.reshape(bsz, nc, t, n_h, p_dim)
    bc = bm.reshape(bsz, nc, t, n_h, SSD_STATE)
    cc = cm.reshape(bsz, nc, t, n_h, SSD_STATE)
    a_cum = jnp.cumsum(log_a.reshape(bsz, nc, t, n_h).transpose(0, 3, 1, 2), axis=-1)
    tri = jnp.tril(jnp.ones((t, t), dtype=bool))
    seg = a_cum[..., :, None] - a_cum[..., None, :]
    l_mat = jnp.exp(jnp.where(tri, seg, -jnp.inf)).astype(x.dtype)
    scores = jnp.einsum('bclhn,bcshn->bhcls', cc, bc) * l_mat
    y_diag = jnp.einsum('bhcls,bcshp->bclhp', scores, xc)
    decay_to_end = jnp.exp(a_cum[..., -1:] - a_cum).astype(x.dtype)
    chunk_states = jnp.einsum('bclhn,bhcl,bclhp->bchpn', bc, decay_to_end, xc)
    chunk_decay = jnp.exp(a_cum[..., -1])

    def step(h, inp):
        s_c, d_c = inp
        return h * d_c[..., None, None] + s_c.astype(jnp.float32), h

    h_final, h_starts = lax.scan(step, h0.astype(jnp.float32),
                                 (jnp.moveaxis(chunk_states, 1, 0), jnp.moveaxis(chunk_decay, 2, 0)))
    decay_in = jnp.exp(a_cum).astype(x.dtype)
    y_off = jnp.einsum('bclhn,cbhpn,bhcl->bclhp', cc, h_starts.astype(x.dtype), decay_in)
    return (y_diag + y_off).reshape(bsz, l_len, n_h, p_dim), h_final.astype(h0.dtype)


def ssd_direction(xh, dt_raw, bm, cm, dt_bias, a_log, d_skip, h0):
    dt = jax.nn.softplus(dt_raw.astype(jnp.float32) + dt_bias.astype(jnp.float32))
    log_a = dt * (-jnp.exp(a_log.astype(jnp.float32)))
    y, h_t = ssd_scan(xh * dt[..., None].astype(xh.dtype), log_a, bm, cm, h0)
    return y + d_skip[:, None] * xh, h_t


def grouped_expert_mlp(x, idx, w, wg, wu, wd):
    n_tok, k = idx.shape
    n_assign = n_tok * k
    flat = idx.reshape(-1)
    order = jnp.argsort(flat)
    sorted_e = flat[order]
    counts = jnp.bincount(flat, length=N_EXPERTS)
    padded = (counts + MOE_BLOCK - 1) // MOE_BLOCK * MOE_BLOCK
    pad_end = jnp.cumsum(padded)
    pad_start = pad_end - padded
    start = jnp.cumsum(counts) - counts
    slot_sorted = pad_start[sorted_e] + (jnp.arange(n_assign) - start[sorted_e])
    n_blocks = -(-n_assign // MOE_BLOCK) + N_EXPERTS
    src_token = jnp.full((n_blocks * MOE_BLOCK,), n_tok, jnp.int32).at[slot_sorted].set((order // k).astype(jnp.int32))
    x_pad = jnp.concatenate([x, jnp.zeros((1, x.shape[1]), x.dtype)], axis=0)
    xs = x_pad[src_token].reshape(n_blocks, MOE_BLOCK, x.shape[1])
    block_e = jnp.minimum(jnp.searchsorted(pad_end, jnp.arange(n_blocks) * MOE_BLOCK, side='right'), N_EXPERTS - 1)

    def run(args):
        xb, e = args
        return (jax.nn.silu(xb @ wg[e]) * (xb @ wu[e])) @ wd[e]

    ys = lax.map(run, (xs, block_e)).reshape(n_blocks * MOE_BLOCK, x.shape[1])
    slot = jnp.zeros((n_assign,), slot_sorted.dtype).at[order].set(slot_sorted).reshape(n_tok, k)
    return jnp.einsum('nk,nkd->nd', w, ys[slot])


def hierarchical_moe(x, lp):
    bsz, l_len, d = x.shape
    xt = x.reshape(-1, d)
    g_logits = (xt @ lp['rg_w'] + lp['rg_b']).astype(jnp.float32)
    _, g_idx = lax.top_k(g_logits, 1)
    p_grp = jnp.take_along_axis(jax.nn.softmax(g_logits, axis=-1), g_idx, axis=-1)
    e_logits = (xt @ lp['re_w'] + lp['re_b']).astype(jnp.float32).reshape(-1, N_EXPERT_GROUPS, EXPERTS_PER_GROUP)
    e_logits = jnp.take_along_axis(e_logits, g_idx[:, :, None], axis=1)[:, 0]
    top_v, top_i = lax.top_k(e_logits, TOP_K)
    weights = (p_grp * jax.nn.softmax(top_v, axis=-1)).astype(x.dtype)
    expert_idx = g_idx * EXPERTS_PER_GROUP + top_i
    y = grouped_expert_mlp(xt, expert_idx, weights, lp['e_wg'], lp['e_wu'], lp['e_wd'])
    return y.reshape(bsz, l_len, d)


def mixer(h, lp, ctx_k, ctx_v, ctx_state, rope):
    bsz, l_len, _ = h.shape
    proj = h @ lp['w_in']
    q, k, v, z, xbc, dt_raw, glu = jnp.split(proj, np.cumsum(IN_SIZES)[:-1].tolist(), axis=-1)
    q = rms_norm(q.reshape(bsz, l_len, ATTN_HEADS, HEAD_DIM), lp['q_norm'])
    k = rms_norm(k.reshape(bsz, l_len, KV_HEADS, HEAD_DIM), lp['k_norm'])
    v = v.reshape(bsz, l_len, KV_HEADS, HEAD_DIM)
    if ctx_k is None:
        attn = block_attention(q, k, v, lp['sink'])
        h0 = jnp.zeros((bsz, 2, SSD_HEADS, SSD_HEAD_DIM, SSD_STATE), h.dtype)
    else:
        cos, sin = rope
        attn = block_attention(apply_axial_rope(q, cos, sin), ctx_k, ctx_v, lp['sink'],
                               apply_axial_rope(k, cos, sin), v)
        h0 = ctx_state
    xbc = jax.nn.silu(depthwise_conv(xbc, lp['ssd_conv_w'], lp['ssd_conv_b']))
    xs, bm, cm = jnp.split(xbc, [SSD_WIDTH, SSD_WIDTH + SSD_GROUPS * SSD_STATE], axis=-1)
    xs = xs.reshape(bsz, l_len, SSD_HEADS, SSD_HEAD_DIM)
    hpg = SSD_HEADS // SSD_GROUPS
    bm = jnp.repeat(bm.reshape(bsz, l_len, SSD_GROUPS, SSD_STATE), hpg, axis=2)
    cm = jnp.repeat(cm.reshape(bsz, l_len, SSD_GROUPS, SSD_STATE), hpg, axis=2)
    y_f, h_f = ssd_direction(xs, dt_raw, bm, cm, lp['dt_bias'][0], lp['a_log'][0], lp['d_skip'][0], h0[:, 0])
    y_b, h_b = ssd_direction(jnp.flip(xs, 1), jnp.flip(dt_raw, 1), jnp.flip(bm, 1), jnp.flip(cm, 1),
                             lp['dt_bias'][1], lp['a_log'][1], lp['d_skip'][1], h0[:, 1])
    y = (y_f + jnp.flip(y_b, 1)).reshape(bsz, l_len, SSD_WIDTH) * jax.nn.silu(z)
    y = rms_norm(y.reshape(bsz, l_len, SSD_GROUPS, -1), lp['ssd_norm'].reshape(SSD_GROUPS, -1)).reshape(bsz, l_len, SSD_WIDTH)
    a, g = jnp.split(glu, 2, axis=-1)
    u = depthwise_conv(a * jax.nn.sigmoid(g), lp['cm_conv_w'], lp['cm_conv_b'])
    u = jax.nn.silu(group_layer_norm(u, lp['cm_norm_w'], lp['cm_norm_b']))
    out = jnp.concatenate([attn, y, u], axis=-1) @ lp['w_out']
    return out, k, v, jnp.stack([h_f, h_b], axis=1)


def trunk_layer(x, cond, lp, ctx_k=None, ctx_v=None, ctx_state=None, rope=None):
    sh1, sc1, g1, sh2, sc2, g2 = modulation(cond, lp['mod_w'], lp['mod_b'])
    mix, k, v, st = mixer(rms_norm(x, lp['norm1']) * (1 + sc1) + sh1, lp, ctx_k, ctx_v, ctx_state, rope)
    x = x + g1 * mix
    x = x + g2 * hierarchical_moe(rms_norm(x, lp['norm2']) * (1 + sc2) + sh2, lp)
    return x, k, v, st


def setup_inputs(seed: int = 0) -> dict:
    key = jax.random.key(seed)
    keys = list(jax.random.split(key, 40))

    def nrm(shape, scale):
        return jax.random.normal(keys.pop(), shape, jnp.float32) * scale

    dt0 = jnp.exp(jax.random.uniform(keys.pop(), (DEPTH, 2, SSD_HEADS), jnp.float32,
                                     minval=math.log(1e-3), maxval=math.log(1e-1)))
    a0 = jax.random.uniform(keys.pop(), (DEPTH, 2, SSD_HEADS), jnp.float32, minval=1.0, maxval=16.0)
    d = D_MODEL
    return {
        'x_prompt': nrm((BATCH, SEQ, d), 1.0),
        'x_sample': nrm((DEC_BATCH, DEC_SEQ, d), 1.0),
        'c': nrm((DEC_BATCH, d), 1.0),
        'cache_k': nrm((DEC_BATCH, DEPTH, PAST_LEN, KV_HEADS, HEAD_DIM), 1.0),
        'cache_v': nrm((DEC_BATCH, DEPTH, PAST_LEN, KV_HEADS, HEAD_DIM), 1.0),
        'state_ssm': nrm((DEC_BATCH, DEPTH, 2, SSD_HEADS, SSD_HEAD_DIM, SSD_STATE), 0.3),
        'c_ctx': nrm((d,), 1.0),
        'mod_w': nrm((DEPTH, d, 6 * d), 0.5 * d ** -0.5),
        'mod_b': nrm((DEPTH, 6 * d), 0.02),
        'norm1_w': 1.0 + nrm((DEPTH, d), 0.02),
        'norm2_w': 1.0 + nrm((DEPTH, d), 0.02),
        'w_in': nrm((DEPTH, d, IN_COLS), d ** -0.5),
        'q_norm_w': 1.0 + nrm((DEPTH, HEAD_DIM), 0.02),
        'k_norm_w': 1.0 + nrm((DEPTH, HEAD_DIM), 0.02),
        'attn_sink': nrm((DEPTH, ATTN_HEADS), 0.5),
        'ssd_conv_w': nrm((DEPTH, SSD_CONV, SSD_XBC), SSD_CONV ** -0.5),
        'ssd_conv_b': nrm((DEPTH, SSD_XBC), 0.02),
        'ssd_dt_bias': dt0 + jnp.log(-jnp.expm1(-dt0)),
        'ssd_a_log': jnp.log(a0),
        'ssd_d': 1.0 + nrm((DEPTH, 2, SSD_HEADS), 0.1),
        'ssd_norm_w': 1.0 + nrm((DEPTH, SSD_WIDTH), 0.02),
        'cm_conv_w': nrm((DEPTH, CONV_KERNEL, CONV_WIDTH), CONV_KERNEL ** -0.5),
        'cm_conv_b': nrm((DEPTH, CONV_WIDTH), 0.02),
        'cm_norm_w': 1.0 + nrm((DEPTH, CONV_WIDTH), 0.02),
        'cm_norm_b': nrm((DEPTH, CONV_WIDTH), 0.02),
        'w_out': nrm((DEPTH, MIX_WIDTH, d), MIX_WIDTH ** -0.5),
        'router_group_w': nrm((DEPTH, d, N_EXPERT_GROUPS), d ** -0.5),
        'router_group_b': nrm((DEPTH, N_EXPERT_GROUPS), 0.01),
        'router_expert_w': nrm((DEPTH, d, N_EXPERTS), d ** -0.5),
        'router_expert_b': nrm((DEPTH, N_EXPERTS), 0.01),
        'expert_w_gate': nrm((DEPTH, N_EXPERTS, d, EXPERT_FF), d ** -0.5),
        'expert_w_up': nrm((DEPTH, N_EXPERTS, d, EXPERT_FF), d ** -0.5),
        'expert_w_down': nrm((DEPTH, N_EXPERTS, EXPERT_FF, d), EXPERT_FF ** -0.5),
    }


def reference(x_prompt, x_sample, c, cache_k, cache_v, state_ssm, c_ctx, mod_w, mod_b, norm1_w, norm2_w,
              w_in, q_norm_w, k_norm_w, attn_sink, ssd_conv_w, ssd_conv_b, ssd_dt_bias, ssd_a_log, ssd_d,
              ssd_norm_w, cm_conv_w, cm_conv_b, cm_norm_w, cm_norm_b, w_out, router_group_w, router_group_b,
              router_expert_w, router_expert_b, expert_w_gate, expert_w_up, expert_w_down):
    rope = axial_rope(x_sample.shape[1])
    y_prompt, y_sample = x_prompt, x_sample
    ks, vs, sts = [], [], []
    for l in range(DEPTH):
        lp = {
            'mod_w': mod_w[l], 'mod_b': mod_b[l], 'norm1': norm1_w[l], 'norm2': norm2_w[l],
            'w_in': w_in[l], 'q_norm': q_norm_w[l], 'k_norm': k_norm_w[l], 'sink': attn_sink[l],
            'ssd_conv_w': ssd_conv_w[l], 'ssd_conv_b': ssd_conv_b[l], 'dt_bias': ssd_dt_bias[l],
            'a_log': ssd_a_log[l], 'd_skip': ssd_d[l], 'ssd_norm': ssd_norm_w[l],
            'cm_conv_w': cm_conv_w[l], 'cm_conv_b': cm_conv_b[l], 'cm_norm_w': cm_norm_w[l],
            'cm_norm_b': cm_norm_b[l], 'w_out': w_out[l],
            'rg_w': router_group_w[l], 'rg_b': router_group_b[l],
            're_w': router_expert_w[l], 're_b': router_expert_b[l],
            'e_wg': expert_w_gate[l], 'e_wu': expert_w_up[l], 'e_wd': expert_w_down[l],
        }
        y_prompt, k_l, v_l, st_l = trunk_layer(y_prompt, c_ctx, lp)
        ks.append(k_l)
        vs.append(v_l)
        sts.append(st_l)
        y_sample, _, _, _ = trunk_layer(y_sample, c, lp, cache_k[:, l], cache_v[:, l], state_ssm[:, l], rope)
    new_cache_k = jnp.stack(ks, axis=1)
    new_cache_v = jnp.stack(vs, axis=1)
    new_state_ssm = jnp.stack(sts, axis=1)
    return (y_prompt, y_sample, new_cache_k, new_cache_v, new_state_ssm)
```

```python
import functools
import math

import jax
import jax.numpy as jnp
from jax import lax
from jax.experimental import pallas as pl
from jax.experimental.pallas import tpu as pltpu

D_MODEL = 1024
GRID_W = 64
HEAD_DIM = 64
ATTN_HEADS = 4
KV_HEADS = 2
ATTN_WIDTH = ATTN_HEADS * HEAD_DIM
KV_WIDTH = KV_HEADS * HEAD_DIM
WINDOW = 128
ATTN_BLOCK = 128
ROPE_THETA = 10000.0
SSD_HEAD_DIM = 64
SSD_WIDTH = 512
SSD_HEADS = 8
SSD_GROUPS = 2
SSD_STATE = 64
SSD_CONV = 5
SSD_CHUNK = 128
SSD_XBC = SSD_WIDTH + 2 * SSD_GROUPS * SSD_STATE
CONV_WIDTH = 256
CONV_KERNEL = 31
CONV_GROUPS = 4
N_EXPERT_GROUPS = 4
EXPERTS_PER_GROUP = 8
N_EXPERTS = 32
EXPERT_FF = 256
EPS = 1e-6

LANES = 128
SUBLANES = 8
VMEM_LIMIT = 52 * 1024 * 1024

MXU_DTYPE = jnp.bfloat16
TM_PROJ = 256
TM_OUT = 2 * SSD_CHUNK
MOE_BLK = 256
TM_COMB = 256
MOD_TN = 1536
HALO_SSD = 8
HALO_CM = 16
COL_Q = 0
COL_K = ATTN_WIDTH
COL_V = COL_K + KV_WIDTH
COL_Z = COL_V + KV_WIDTH
COL_XBC = COL_Z + SSD_WIDTH
COL_GLU = COL_XBC + SSD_XBC
COL_DT = COL_GLU + 2 * CONV_WIDTH
PROJ_COLS = COL_DT + LANES


def _f32(x):
    return x.astype(jnp.float32)


def _mx(x):
    return x.astype(MXU_DTYPE)


def _dot(a, b):
    return jnp.dot(a, b, preferred_element_type=jnp.float32)


def _dot_nt(a, b):
    return lax.dot_general(a, b, (((1,), (1,)), ((), ())), preferred_element_type=jnp.float32)


def _split2(x):
    hi = _mx(x)
    lo = _mx(x - _f32(hi))
    return hi, lo


def _split3(x):
    hi = _mx(x)
    r = x - _f32(hi)
    mid = _mx(r)
    lo = _mx(r - _f32(mid))
    return hi, mid, lo


def _dot_sel2(x, sel):
    hi, lo = _split2(x)
    return _dot(hi, sel) + _dot(lo, sel)


def _sel_dot3(sel, x):
    hi, mid, lo = _split3(x)
    return _dot(sel, hi) + _dot(sel, mid) + _dot(sel, lo)


def _silu(x):
    return x * jax.nn.sigmoid(x)


def _softplus(x):
    return jnp.maximum(x, 0.0) + jnp.log(1.0 + jnp.exp(-jnp.abs(x)))


def _cparams(n_axes=1):
    return pltpu.CompilerParams(dimension_semantics=("arbitrary",) * n_axes,
                                vmem_limit_bytes=VMEM_LIMIT)


def _full(shape):
    nd = len(shape)
    return pl.BlockSpec(shape, lambda *_: (0,) * nd)


def _mod_kernel(c_ref, w_ref, b_ref, o_ref):
    s = _silu(c_ref[...])
    o_ref[0] = _dot(_mx(s), _mx(w_ref[0])) + b_ref[0]


def _modulation(cond8, mod_w, mod_b):
    depth, d, n6 = mod_w.shape
    return pl.pallas_call(
        _mod_kernel,
        out_shape=jax.ShapeDtypeStruct((depth, SUBLANES, n6), jnp.float32),
        grid=(depth, n6 // MOD_TN),
        in_specs=[pl.BlockSpec((SUBLANES, d), lambda l, j: (0, 0)),
                  pl.BlockSpec((1, d, MOD_TN), lambda l, j: (l, 0, j)),
                  pl.BlockSpec((1, 1, MOD_TN), lambda l, j: (l, 0, j))],
        out_specs=pl.BlockSpec((1, SUBLANES, MOD_TN), lambda l, j: (l, 0, j)),
        compiler_params=_cparams(2),
        name="modulation",
    )(cond8, mod_w, mod_b.reshape(depth, 1, n6))


def _rope128(x, cos, sin, first_half):
    partner = jnp.where(first_half, pltpu.roll(x, LANES - 16, 1), pltpu.roll(x, 16, 1))
    return x * cos + partner * sin


def _inproj_kernel(x_ref, m_ref, n1_ref, w_ref, qkw_ref, bd_ref, cos_ref, sin_ref,
                   q_ref, kr_ref, kn_ref, v_ref, z_ref, xbc_ref, glu_ref, dt_ref):
    x = x_ref[...]
    xn = x * lax.rsqrt(jnp.mean(x * x, axis=-1, keepdims=True) + EPS) * n1_ref[...]
    h = xn * (1.0 + m_ref[0, 1:2, :]) + m_ref[0, 0:1, :]
    p = _dot(_mx(h), w_ref[...])
    qk = p[:, COL_Q:COL_V]
    seg = _dot_sel2(qk * qk, bd_ref[...])
    qk = qk * lax.rsqrt(seg * (1.0 / HEAD_DIM) + EPS) * qkw_ref[...]
    cos = cos_ref[...]
    sin = sin_ref[...]
    lane = lax.broadcasted_iota(jnp.int32, cos.shape, 1)
    first_half = (lane % 32) < 16
    scale = HEAD_DIM ** -0.5
    for c in range(ATTN_WIDTH // LANES):
        qc = _rope128(qk[:, c * LANES:(c + 1) * LANES], cos, sin, first_half)
        q_ref[:, c * LANES:(c + 1) * LANES] = (qc * scale).astype(q_ref.dtype)
    kn = qk[:, COL_K:COL_V]
    kn_ref[...] = kn
    kr_ref[...] = _rope128(kn, cos, sin, first_half).astype(kr_ref.dtype)
    v_ref[...] = p[:, COL_V:COL_Z]
    z_ref[...] = p[:, COL_Z:COL_XBC]
    xbc_ref[...] = p[:, COL_XBC:COL_GLU]
    glu_ref[...] = p[:, COL_GLU:COL_DT]
    dt_ref[...] = p[:, COL_DT:PROJ_COLS]


def _in_projection(x, mod_l, n1, w_packed, qkw, bd, cos_t, sin_t, n_prompt, sample_len):
    n = x.shape[0]
    tm = TM_PROJ
    ident_blk = sample_len // tm

    def cond_row(t):
        s = t * tm
        return jnp.where(s < n_prompt, 0, 1 + (s - n_prompt) // sample_len)

    def rope_blk(t):
        s = t * tm
        return jnp.where(s < n_prompt, ident_blk, ((s - n_prompt) % sample_len) // tm)

    def rows(width):
        return pl.BlockSpec((tm, width), lambda t: (t, 0))

    widths = (ATTN_WIDTH, KV_WIDTH, KV_WIDTH, KV_WIDTH, SSD_WIDTH, SSD_XBC, 2 * CONV_WIDTH, LANES)
    dtypes = (MXU_DTYPE, MXU_DTYPE, jnp.float32, jnp.float32, jnp.float32, jnp.float32, jnp.float32,
              jnp.float32)
    return pl.pallas_call(
        _inproj_kernel,
        out_shape=[jax.ShapeDtypeStruct((n, w), dt) for w, dt in zip(widths, dtypes)],
        grid=(n // tm,),
        in_specs=[rows(D_MODEL),
                  pl.BlockSpec((1, 6, D_MODEL), lambda t: (cond_row(t), 0, 0)),
                  _full((1, D_MODEL)),
                  _full((D_MODEL, PROJ_COLS)),
                  _full((1, COL_V)),
                  _full((COL_V, COL_V)),
                  pl.BlockSpec((tm, LANES), lambda t: (rope_blk(t), 0)),
                  pl.BlockSpec((tm, LANES), lambda t: (rope_blk(t), 0))],
        out_specs=[rows(w) for w in widths],
        compiler_params=_cparams(1),
        name="in_projection",
    )(x, mod_l, n1, w_packed, qkw, bd, cos_t, sin_t)


def _attn_kernel(sink_ref, q_ref, kc_ref, vc_ref, *rest, local, blocks_per_seq):
    if local:
        kp_ref, kcur_ref, knx_ref, vp_ref, vcur_ref, vnx_ref, o_ref = rest
    else:
        (o_ref,) = rest
    blk = ATTN_BLOCK
    j = pl.program_id(0) % blocks_per_seq
    q = q_ref[...]
    row = lax.broadcasted_iota(jnp.int32, (2 * blk, 1), 0)
    if local:
        qo = lax.broadcasted_iota(jnp.int32, (2 * blk, 3 * blk), 0) % blk
        kpos = lax.broadcasted_iota(jnp.int32, (2 * blk, 3 * blk), 1) - blk
        pos = j * blk + kpos
        valid = (jnp.abs(kpos - qo) <= WINDOW) & (pos >= 0) & (pos < blocks_per_seq * blk)
    for g in range(KV_HEADS):
        sl = slice(g * HEAD_DIM, (g + 1) * HEAD_DIM)
        q2 = jnp.concatenate([q[:, (2 * g) * HEAD_DIM:(2 * g + 1) * HEAD_DIM],
                              q[:, (2 * g + 1) * HEAD_DIM:(2 * g + 2) * HEAD_DIM]], axis=0)
        s_ctx = _dot_nt(q2, _mx(kc_ref[:, sl]))
        sink = jnp.where(row < blk, sink_ref[2 * g], sink_ref[2 * g + 1])
        m = jnp.maximum(jnp.max(s_ctx, axis=-1, keepdims=True), sink)
        if local:
            kl = jnp.concatenate([_mx(kp_ref[:, sl]), _mx(kcur_ref[:, sl]), _mx(knx_ref[:, sl])], axis=0)
            vl = jnp.concatenate([_mx(vp_ref[:, sl]), _mx(vcur_ref[:, sl]), _mx(vnx_ref[:, sl])], axis=0)
            s_loc = jnp.where(valid, _dot_nt(q2, kl), -jnp.inf)
            m = jnp.maximum(m, jnp.max(s_loc, axis=-1, keepdims=True))
        p_ctx = jnp.exp(s_ctx - m)
        den = jnp.sum(p_ctx, axis=-1, keepdims=True) + jnp.exp(sink - m)
        o = _dot(_mx(p_ctx), _mx(vc_ref[:, sl]))
        if local:
            p_loc = jnp.exp(s_loc - m)
            den = den + jnp.sum(p_loc, axis=-1, keepdims=True)
            o = o + _dot(_mx(p_loc), vl)
        o = o / den
        o_ref[:, (2 * g) * HEAD_DIM:(2 * g + 1) * HEAD_DIM] = o[:blk]
        o_ref[:, (2 * g + 1) * HEAD_DIM:(2 * g + 2) * HEAD_DIM] = o[blk:]


def _attention(sink, q, k, v, kctx, vctx, *, tok_off, n_tok, seq_len, ctx_len, local):
    blk = ATTN_BLOCK
    bps = seq_len // blk
    off = tok_off // blk
    nb = n_tok // blk
    kctx_arr, kctx_off = kctx
    vctx_arr, vctx_off = vctx
    cb_k = kctx_off // ctx_len
    cb_v = vctx_off // ctx_len

    def prev_blk(i):
        return off + (i // bps) * bps + jnp.maximum(i % bps - 1, 0)

    def next_blk(i):
        return off + (i // bps) * bps + jnp.minimum(i % bps + 1, bps - 1)

    in_specs = [pl.BlockSpec(memory_space=pltpu.SMEM),
                pl.BlockSpec((blk, ATTN_WIDTH), lambda i: (off + i, 0)),
                pl.BlockSpec((ctx_len, KV_WIDTH), lambda i: (cb_k + i // bps, 0)),
                pl.BlockSpec((ctx_len, KV_WIDTH), lambda i: (cb_v + i // bps, 0))]
    args = [sink, q, kctx_arr, vctx_arr]
    if local:
        for arr in (k, v):
            in_specs += [pl.BlockSpec((blk, KV_WIDTH), lambda i: (prev_blk(i), 0)),
                         pl.BlockSpec((blk, KV_WIDTH), lambda i: (off + i, 0)),
                         pl.BlockSpec((blk, KV_WIDTH), lambda i: (next_blk(i), 0))]
            args += [arr, arr, arr]
    return pl.pallas_call(
        functools.partial(_attn_kernel, local=local, blocks_per_seq=bps),
        out_shape=jax.ShapeDtypeStruct((n_tok, ATTN_WIDTH), jnp.float32),
        grid=(nb,),
        in_specs=in_specs,
        out_specs=pl.BlockSpec((blk, ATTN_WIDTH), lambda i: (i, 0)),
        compiler_params=_cparams(1),
        name="attention_local" if local else "attention_ctx",
    )(*args)


def _ssd_local_kernel(xbc_ref, xp_ref, xn_ref, glu_ref, gp_ref, gn_ref, dt_ref,
                      cw_ref, cb_ref, dtb_ref, alog_ref, dsum_ref, ef_ref, eb_ref, tril_ref, triu_ref,
                      mw_ref, mb_ref, nw_ref, nb_ref, bd_ref,
                      y_ref, st_ref, cdec_ref, cm_ref, din_ref, u_ref,
                      xext, gext, *, n_prompt_chunks, prompt_cps, sample_cps):
    t = SSD_CHUNK
    g = pl.program_id(0)
    in_prompt = g < n_prompt_chunks
    cps = jnp.where(in_prompt, prompt_cps, sample_cps)
    cidx = jnp.where(in_prompt, g % prompt_cps, (g - n_prompt_chunks) % sample_cps)
    not_first = (cidx > 0).astype(jnp.float32)
    not_last = (cidx < cps - 1).astype(jnp.float32)

    xext[0:HALO_SSD, :] = xp_ref[...] * not_first
    xext[HALO_SSD:HALO_SSD + t, :] = xbc_ref[...]
    xext[HALO_SSD + t:HALO_SSD + t + HALO_SSD, :] = xn_ref[...] * not_last
    half = (SSD_CONV - 1) // 2
    acc = jnp.zeros((t, SSD_XBC), jnp.float32) + cb_ref[...]
    for k in range(SSD_CONV):
        acc = acc + xext[pl.ds(HALO_SSD - half + k, t), :] * cw_ref[k:k + 1, :]
    xc = _silu(acc)
    xs = xc[:, :SSD_WIDTH]
    bm = xc[:, SSD_WIDTH:SSD_WIDTH + SSD_GROUPS * SSD_STATE]
    cm = xc[:, SSD_WIDTH + SSD_GROUPS * SSD_STATE:]
    cm_ref[...] = cm

    lane = lax.broadcasted_iota(jnp.int32, (t, LANES), 1)
    fwd_lane = lane < SSD_HEADS
    dt = _softplus(dt_ref[...] + dtb_ref[...])
    la = dt * (-jnp.exp(alog_ref[...]))
    cs = jnp.where(fwd_lane, _sel_dot3(tril_ref[...], la), _sel_dot3(triu_ref[...], la))
    tot = jnp.where(fwd_lane[0:1], cs[t - 1:t, :], cs[0:1, :])
    din = jnp.exp(cs)
    din_ref[...] = din
    dec = jnp.exp(tot - cs)
    cs_row = cs.T
    ef = ef_ref[...]
    eb = eb_ref[...]
    cd = jnp.broadcast_to(jnp.exp(tot), (SUBLANES, LANES))
    cdec_ref[0] = jnp.concatenate([_dot_sel2(cd, ef)[0:1], _dot_sel2(cd, eb)[0:1],
                                   jnp.zeros((SUBLANES - 2, SSD_WIDTH), jnp.float32)], axis=0)

    bm_t = bm.T
    li = lax.broadcasted_iota(jnp.int32, (t, t), 0)
    si = lax.broadcasted_iota(jnp.int32, (t, t), 1)
    scores = [_dot_nt(_mx(cm[:, gi * SSD_STATE:(gi + 1) * SSD_STATE]),
                      _mx(bm[:, gi * SSD_STATE:(gi + 1) * SSD_STATE])) for gi in range(SSD_GROUPS)]
    hpg = SSD_HEADS // SSD_GROUPS
    gw = hpg * SSD_HEAD_DIM
    y_heads = [None] * SSD_HEADS
    for d, e_d in enumerate((ef, eb)):
        xdt = xs * _dot_sel2(dt, e_d)
        xdec = _mx(xdt * _dot_sel2(dec, e_d))
        st_ref[0, d] = jnp.concatenate(
            [_dot(_mx(bm_t[gi * SSD_STATE:(gi + 1) * SSD_STATE, :]), xdec[:, gi * gw:(gi + 1) * gw])
             for gi in range(SSD_GROUPS)], axis=1)
        mask = (si <= li) if d == 0 else (si >= li)
        for h in range(SSD_HEADS):
            c = d * SSD_HEADS + h
            seg = cs[:, c:c + 1] - cs_row[c:c + 1, :]
            lmat = jnp.exp(jnp.where(mask, seg, -jnp.inf))
            yh = _dot(_mx(scores[h // hpg] * lmat), _mx(xdt[:, h * SSD_HEAD_DIM:(h + 1) * SSD_HEAD_DIM]))
            y_heads[h] = yh if y_heads[h] is None else y_heads[h] + yh
    y_ref[...] = jnp.concatenate(y_heads, axis=1) + xs * dsum_ref[...]

    def glu(v):
        return v[:, :CONV_WIDTH] * jax.nn.sigmoid(v[:, CONV_WIDTH:])

    gext[0:HALO_CM, :] = glu(gp_ref[...]) * not_first
    gext[HALO_CM:HALO_CM + t, :] = glu(glu_ref[...])
    gext[HALO_CM + t:HALO_CM + t + HALO_CM, :] = glu(gn_ref[...]) * not_last
    halfc = (CONV_KERNEL - 1) // 2
    u = jnp.zeros((t, CONV_WIDTH), jnp.float32) + mb_ref[...]
    for k in range(CONV_KERNEL):
        u = u + gext[pl.ds(HALO_CM - halfc + k, t), :] * mw_ref[k:k + 1, :]
    gsz = CONV_WIDTH // CONV_GROUPS
    bd = bd_ref[...]
    uc = u - _dot_sel2(u, bd) * (1.0 / gsz)
    un = uc * lax.rsqrt(_dot_sel2(uc * uc, bd) * (1.0 / gsz) + EPS)
    u_ref[...] = _silu(un * nw_ref[...] + nb_ref[...])


def _ssd_local(xbc, glu, dt, pr, consts, *, n_prompt, prompt_len, sample_len):
    n = xbc.shape[0]
    t = SSD_CHUNK
    nch = n // t
    r8 = t // HALO_SSD
    r16 = t // HALO_CM
    kern = functools.partial(_ssd_local_kernel, n_prompt_chunks=n_prompt // t,
                             prompt_cps=prompt_len // t, sample_cps=sample_len // t)
    out_shape = [jax.ShapeDtypeStruct((n, SSD_WIDTH), jnp.float32),
                 jax.ShapeDtypeStruct((nch, 2, SSD_STATE, SSD_WIDTH), jnp.float32),
                 jax.ShapeDtypeStruct((nch, SUBLANES, SSD_WIDTH), jnp.float32),
                 jax.ShapeDtypeStruct((n, LANES), jnp.float32),
                 jax.ShapeDtypeStruct((n, LANES), jnp.float32),
                 jax.ShapeDtypeStruct((n, CONV_WIDTH), jnp.float32)]
    out_specs = [pl.BlockSpec((t, SSD_WIDTH), lambda g: (g, 0)),
                 pl.BlockSpec((1, 2, SSD_STATE, SSD_WIDTH), lambda g: (g, 0, 0, 0)),
                 pl.BlockSpec((1, SUBLANES, SSD_WIDTH), lambda g: (g, 0, 0)),
                 pl.BlockSpec((t, LANES), lambda g: (g, 0)),
                 pl.BlockSpec((t, LANES), lambda g: (g, 0)),
                 pl.BlockSpec((t, CONV_WIDTH), lambda g: (g, 0))]
    in_specs = [pl.BlockSpec((t, SSD_XBC), lambda g: (g, 0)),
                pl.BlockSpec((HALO_SSD, SSD_XBC), lambda g: (jnp.maximum(g * r8 - 1, 0), 0)),
                pl.BlockSpec((HALO_SSD, SSD_XBC), lambda g: (jnp.minimum((g + 1) * r8, n // HALO_SSD - 1), 0)),
                pl.BlockSpec((t, 2 * CONV_WIDTH), lambda g: (g, 0)),
                pl.BlockSpec((HALO_CM, 2 * CONV_WIDTH), lambda g: (jnp.maximum(g * r16 - 1, 0), 0)),
                pl.BlockSpec((HALO_CM, 2 * CONV_WIDTH),
                             lambda g: (jnp.minimum((g + 1) * r16, n // HALO_CM - 1), 0)),
                pl.BlockSpec((t, LANES), lambda g: (g, 0))]
    params = [pr["ssd_conv_w"], pr["ssd_conv_b"], pr["dtb"], pr["alog"], pr["dsum"],
              consts["ef"], consts["eb"], consts["tril"], consts["triu"],
              pr["cm_conv_w"], pr["cm_conv_b"], pr["cm_norm_w"], pr["cm_norm_b"], consts["bd_cm"]]
    in_specs += [_full(p.shape) for p in params]
    return pl.pallas_call(
        kern,
        out_shape=out_shape,
        grid=(nch,),
        in_specs=in_specs,
        out_specs=out_specs,
        scratch_shapes=[pltpu.VMEM((t + 2 * HALO_SSD, SSD_XBC), jnp.float32),
                        pltpu.VMEM((t + 2 * HALO_CM, CONV_WIDTH), jnp.float32)],
        compiler_params=_cparams(1),
        name="ssd_local",
    )(xbc, xbc, xbc, glu, glu, glu, dt, *params)


def _ssd_scan_kernel(seq_ref, first_ref, last_ref, stf_ref, stb_ref, cdf_ref, cdb_ref, h0f_ref, h0b_ref,
                     hsf_ref, hsb_ref, hff_ref, hfb_ref, hf, hb):
    g = pl.program_id(0)
    gb = pl.num_programs(0) - 1 - g

    @pl.when(first_ref[g] == 1)
    def _():
        hf[...] = h0f_ref[0, 0]

    @pl.when(last_ref[gb] == 1)
    def _():
        hb[...] = h0b_ref[0, 0]

    hsf_ref[0] = hf[...]
    hf_new = hf[...] * cdf_ref[0, 0:1, :] + stf_ref[0, 0]
    hf[...] = hf_new
    hff_ref[0, 0] = hf_new
    hsb_ref[0] = hb[...]
    hb_new = hb[...] * cdb_ref[0, 1:2, :] + stb_ref[0, 0]
    hb[...] = hb_new
    hfb_ref[0, 0] = hb_new


def _ssd_scan(seq_id, first, last, st, cdec, h0):
    nch = st.shape[0]
    nseq = h0.shape[0]
    blk4 = (1, 1, SSD_STATE, SSD_WIDTH)
    grid_spec = pltpu.PrefetchScalarGridSpec(
        num_scalar_prefetch=3,
        grid=(nch,),
        in_specs=[pl.BlockSpec(blk4, lambda g, s, f, l: (g, 0, 0, 0)),
                  pl.BlockSpec(blk4, lambda g, s, f, l: (nch - 1 - g, 1, 0, 0)),
                  pl.BlockSpec((1, SUBLANES, SSD_WIDTH), lambda g, s, f, l: (g, 0, 0)),
                  pl.BlockSpec((1, SUBLANES, SSD_WIDTH), lambda g, s, f, l: (nch - 1 - g, 0, 0)),
                  pl.BlockSpec(blk4, lambda g, s, f, l: (s[g], 0, 0, 0)),
                  pl.BlockSpec(blk4, lambda g, s, f, l: (s[nch - 1 - g], 1, 0, 0))],
        out_specs=[pl.BlockSpec((1, SSD_STATE, SSD_WIDTH), lambda g, s, f, l: (g, 0, 0)),
                   pl.BlockSpec((1, SSD_STATE, SSD_WIDTH), lambda g, s, f, l: (nch - 1 - g, 0, 0)),
                   pl.BlockSpec(blk4, lambda g, s, f, l: (s[g], 0, 0, 0)),
                   pl.BlockSpec(blk4, lambda g, s, f, l: (s[nch - 1 - g], 0, 0, 0))],
        scratch_shapes=[pltpu.VMEM((SSD_STATE, SSD_WIDTH), jnp.float32),
                        pltpu.VMEM((SSD_STATE, SSD_WIDTH), jnp.float32)])
    return pl.pallas_call(
        _ssd_scan_kernel,
        out_shape=[jax.ShapeDtypeStruct((nch, SSD_STATE, SSD_WIDTH), jnp.float32),
                   jax.ShapeDtypeStruct((nch, SSD_STATE, SSD_WIDTH), jnp.float32),
                   jax.ShapeDtypeStruct((nseq, 1, SSD_STATE, SSD_WIDTH), jnp.float32),
                   jax.ShapeDtypeStruct((nseq, 1, SSD_STATE, SSD_WIDTH), jnp.float32)],
        grid_spec=grid_spec,
        compiler_params=_cparams(1),
        name="ssd_scan",
    )(seq_id, first, last, st, st, cdec, cdec, h0, h0)


def _outproj_kernel(x_ref, attn_ref, u_ref, y_ref, z_ref, cm_ref, din_ref, hsf_ref, hsb_ref, m_ref,
                    sw_ref, wo_ref, n2_ref, rwh_ref, rwl_ref, rb_ref, ef_ref, eb_ref,
                    x1_ref, xm_ref, ri_ref, rw_ref):
    t = SSD_CHUNK
    hpg = SSD_HEADS // SSD_GROUPS
    gw = hpg * SSD_HEAD_DIM
    ys = []
    for c in range(TM_OUT // t):
        rs = slice(c * t, (c + 1) * t)
        cmc = cm_ref[rs, :]
        dinc = din_ref[rs, :]
        y = y_ref[rs, :]
        for hs_ref, e_ref in ((hsf_ref, ef_ref), (hsb_ref, eb_ref)):
            hs = hs_ref[c]
            yoff = jnp.concatenate(
                [_dot(_mx(cmc[:, gi * SSD_STATE:(gi + 1) * SSD_STATE]), _mx(hs[:, gi * gw:(gi + 1) * gw]))
                 for gi in range(SSD_GROUPS)], axis=1)
            y = y + yoff * _dot_sel2(dinc, e_ref[...])
        ys.append(y)
    y = jnp.concatenate(ys, axis=0) * _silu(z_ref[...])
    nrm = []
    for gi in range(SSD_GROUPS):
        yg = y[:, gi * gw:(gi + 1) * gw]
        nrm.append(yg * lax.rsqrt(jnp.mean(yg * yg, axis=-1, keepdims=True) + EPS))
    y = jnp.concatenate(nrm, axis=1) * sw_ref[...]
    o1 = ATTN_WIDTH
    o2 = ATTN_WIDTH + SSD_WIDTH
    mix = (_dot(_mx(attn_ref[...]), wo_ref[0:o1, :]) + _dot(_mx(y), wo_ref[o1:o2, :])
           + _dot(_mx(u_ref[...]), wo_ref[o2:, :]))
    x1 = x_ref[...] + m_ref[0, 2:3, :] * mix
    x1_ref[...] = x1
    xm = (x1 * lax.rsqrt(jnp.mean(x1 * x1, axis=-1, keepdims=True) + EPS) * n2_ref[...]
          * (1.0 + m_ref[0, 4:5, :]) + m_ref[0, 3:4, :])
    xm_ref[...] = xm
    xh, xl = _split2(xm)
    lg = _dot(xh, rwh_ref[...]) + _dot(xh, rwl_ref[...]) + _dot(xl, rwh_ref[...]) + rb_ref[...]
    lane_i = lax.broadcasted_iota(jnp.int32, lg.shape, 1)
    lane = lane_i.astype(jnp.float32)
    ninf = -jnp.inf
    big = float(LANES)
    is_g = lane_i < N_EXPERT_GROUPS
    gmax = jnp.max(jnp.where(is_g, lg, ninf), axis=-1, keepdims=True)
    gidx = jnp.min(jnp.where(is_g & (lg == gmax), lane, big), axis=-1, keepdims=True)
    p_grp = 1.0 / jnp.sum(jnp.where(is_g, jnp.exp(lg - gmax), 0.0), axis=-1, keepdims=True)
    lo = N_EXPERT_GROUPS + EXPERTS_PER_GROUP * gidx
    in_e = (lane >= lo) & (lane < lo + EXPERTS_PER_GROUP)
    v1 = jnp.max(jnp.where(in_e, lg, ninf), axis=-1, keepdims=True)
    i1 = jnp.min(jnp.where(in_e & (lg == v1), lane, big), axis=-1, keepdims=True)
    rest = in_e & (lane != i1)
    v2 = jnp.max(jnp.where(rest, lg, ninf), axis=-1, keepdims=True)
    i2 = jnp.min(jnp.where(rest & (lg == v2), lane, big), axis=-1, keepdims=True)
    e2 = jnp.exp(v2 - v1)
    w1 = p_grp * (1.0 / (1.0 + e2))
    w2 = p_grp * (e2 / (1.0 + e2))
    e_first = (i1 - N_EXPERT_GROUPS).astype(jnp.int32)
    e_second = (i2 - N_EXPERT_GROUPS).astype(jnp.int32)
    ri_ref[...] = jnp.where(lane_i == 0, e_first, jnp.where(lane_i == 1, e_second, 0))
    rw_ref[...] = jnp.where(lane_i == 0, w1, jnp.where(lane_i == 1, w2, 0.0))


def _out_projection(x, attn, u, ydiag, z, cm, din, hsf, hsb, mod_l, pr, consts, *, n_prompt, sample_len):
    n = x.shape[0]
    tm = TM_OUT
    cpt = tm // SSD_CHUNK

    def cond_row(t):
        s = t * tm
        return jnp.where(s < n_prompt, 0, 1 + (s - n_prompt) // sample_len)

    def rows(width):
        return pl.BlockSpec((tm, width), lambda t: (t, 0))

    params = [pr["ssd_norm_w"], pr["w_out"], pr["norm2"], pr["rw_hi"], pr["rw_lo"], pr["rb"],
              consts["ef"], consts["eb"]]
    return pl.pallas_call(
        _outproj_kernel,
        out_shape=[jax.ShapeDtypeStruct((n, D_MODEL), jnp.float32),
                   jax.ShapeDtypeStruct((n, D_MODEL), jnp.float32),
                   jax.ShapeDtypeStruct((n, LANES), jnp.int32),
                   jax.ShapeDtypeStruct((n, LANES), jnp.float32)],
        grid=(n // tm,),
        in_specs=[rows(D_MODEL), rows(ATTN_WIDTH), rows(CONV_WIDTH), rows(SSD_WIDTH), rows(SSD_WIDTH),
                  rows(LANES), rows(LANES),
                  pl.BlockSpec((cpt, SSD_STATE, SSD_WIDTH), lambda t: (t, 0, 0)),
                  pl.BlockSpec((cpt, SSD_STATE, SSD_WIDTH), lambda t: (t, 0, 0)),
                  pl.BlockSpec((1, 6, D_MODEL), lambda t: (cond_row(t), 0, 0))]
        + [_full(p.shape) for p in params],
        out_specs=[rows(D_MODEL), rows(D_MODEL), rows(LANES), rows(LANES)],
        compiler_params=_cparams(1),
        name="out_projection",
    )(x, attn, u, ydiag, z, cm, din, hsf, hsb, mod_l, *params)


def _route_slots(ridx):
    n = ridx.shape[0]
    n_assign = 2 * n
    blk = MOE_BLK
    flat = ridx[:, :2].reshape(-1)
    onehot = (flat[:, None] == jnp.arange(N_EXPERTS, dtype=jnp.int32)[None, :]).astype(jnp.int32)
    csum = jnp.cumsum(onehot, axis=0)
    counts = csum[-1]
    padded = (counts + blk - 1) // blk * blk
    pad_end = jnp.cumsum(padded)
    pad_start = pad_end - padded
    slot = jnp.sum(onehot * (csum - 1 + pad_start[None, :]), axis=1).astype(jnp.int32)
    n_blocks = n_assign // blk + N_EXPERTS
    src = jnp.zeros((n_blocks * blk,), jnp.int32).at[slot].set(jnp.arange(n_assign, dtype=jnp.int32) // 2)
    block_e = jnp.minimum(jnp.searchsorted(pad_end, jnp.arange(n_blocks, dtype=jnp.int32) * blk, side="right"),
                          N_EXPERTS - 1).astype(jnp.int32)
    n_valid = (pad_end[-1] // blk).astype(jnp.int32).reshape(1)
    return slot, src, block_e, n_valid


def _gather_rows(idx_ref, base, src_hbm, dst, sem, n_rows):
    def body(j, carry):
        r = idx_ref[base + j]
        pltpu.make_async_copy(src_hbm.at[pl.ds(r, 1)], dst.at[pl.ds(j, 1)], sem).start()
        return carry

    lax.fori_loop(0, n_rows, body, 0)


def _wait_rows(src_hbm, dst, sem, n_rows):
    pltpu.make_async_copy(src_hbm.at[pl.ds(0, n_rows)], dst, sem).wait()


def _experts_kernel(be_ref, src_ref, nv_ref, x_hbm, wg_ref, wu_ref, wd_ref, o_ref,
                    xbuf, sem, wg_c, wu_c, wd_c):
    b = pl.program_id(0)
    nv = nv_ref[0]
    blk = MOE_BLK

    @pl.when(b == 0)
    def _():
        _gather_rows(src_ref, 0, x_hbm, xbuf.at[0], sem.at[0], blk)

    @pl.when(b + 1 < nv)
    def _():
        nxt = (b + 1) % 2
        _gather_rows(src_ref, (b + 1) * blk, x_hbm, xbuf.at[nxt], sem.at[nxt], blk)

    @pl.when((b == 0) | (be_ref[b] != be_ref[jnp.maximum(b - 1, 0)]))
    def _():
        wg_c[...] = _mx(wg_ref[0, 0])
        wu_c[...] = _mx(wu_ref[0, 0])
        wd_c[...] = _mx(wd_ref[0, 0])

    @pl.when(b < nv)
    def _():
        cur = b % 2
        _wait_rows(x_hbm, xbuf.at[cur], sem.at[cur], blk)
        xb = _mx(xbuf[cur])
        hid = _silu(_dot(xb, wg_c[...])) * _dot(xb, wu_c[...])
        o_ref[...] = _dot(_mx(hid), wd_c[...])

    @pl.when(b >= nv)
    def _():
        o_ref[...] = jnp.zeros_like(o_ref)


def _experts(block_e, src, n_valid, xm, wg, wu, wd, layer):
    blk = MOE_BLK
    n_blocks = block_e.shape[0]
    grid_spec = pltpu.PrefetchScalarGridSpec(
        num_scalar_prefetch=3,
        grid=(n_blocks,),
        in_specs=[pl.BlockSpec(memory_space=pl.ANY),
                  pl.BlockSpec((1, 1, D_MODEL, EXPERT_FF), lambda b, be, s, nv: (layer, be[b], 0, 0)),
                  pl.BlockSpec((1, 1, D_MODEL, EXPERT_FF), lambda b, be, s, nv: (layer, be[b], 0, 0)),
                  pl.BlockSpec((1, 1, EXPERT_FF, D_MODEL), lambda b, be, s, nv: (layer, be[b], 0, 0))],
        out_specs=pl.BlockSpec((blk, D_MODEL), lambda b, be, s, nv: (b, 0)),
        scratch_shapes=[pltpu.VMEM((2, blk, D_MODEL), jnp.float32),
                        pltpu.SemaphoreType.DMA((2,)),
                        pltpu.VMEM((D_MODEL, EXPERT_FF), MXU_DTYPE),
                        pltpu.VMEM((D_MODEL, EXPERT_FF), MXU_DTYPE),
                        pltpu.VMEM((EXPERT_FF, D_MODEL), MXU_DTYPE)])
    return pl.pallas_call(
        _experts_kernel,
        out_shape=jax.ShapeDtypeStruct((n_blocks * blk, D_MODEL), jnp.float32),
        grid_spec=grid_spec,
        compiler_params=_cparams(1),
        name="experts",
    )(block_e, src, n_valid, xm, wg, wu, wd)


def _combine_kernel(slot_ref, x1_ref, m_ref, rw_ref, ys_hbm, o_ref, ybuf, sem):
    i = pl.program_id(0)
    n_steps = pl.num_programs(0)
    rows = 2 * TM_COMB

    @pl.when(i == 0)
    def _():
        _gather_rows(slot_ref, 0, ys_hbm, ybuf.at[0], sem.at[0], rows)

    @pl.when(i + 1 < n_steps)
    def _():
        nxt = (i + 1) % 2
        _gather_rows(slot_ref, (i + 1) * rows, ys_hbm, ybuf.at[nxt], sem.at[nxt], rows)

    cur = i % 2
    _wait_rows(ys_hbm, ybuf.at[cur], sem.at[cur], rows)
    rw = rw_ref[...]
    ya = ybuf[cur, 0:TM_COMB, :]
    yb = ybuf[cur, TM_COMB:rows, :]
    o_ref[...] = x1_ref[...] + m_ref[0, 5:6, :] * (rw[:, 0:1] * ya + rw[:, 1:2] * yb)


def _combine(slot_km, x1, mod_l, rw, ys, *, n_prompt, sample_len):
    n = x1.shape[0]
    tm = TM_COMB

    def cond_row(t, *_):
        s = t * tm
        return jnp.where(s < n_prompt, 0, 1 + (s - n_prompt) // sample_len)

    grid_spec = pltpu.PrefetchScalarGridSpec(
        num_scalar_prefetch=1,
        grid=(n // tm,),
        in_specs=[pl.BlockSpec((tm, D_MODEL), lambda t, s: (t, 0)),
                  pl.BlockSpec((1, 6, D_MODEL), lambda t, s: (cond_row(t), 0, 0)),
                  pl.BlockSpec((tm, LANES), lambda t, s: (t, 0)),
                  pl.BlockSpec(memory_space=pl.ANY)],
        out_specs=pl.BlockSpec((tm, D_MODEL), lambda t, s: (t, 0)),
        scratch_shapes=[pltpu.VMEM((2, 2 * tm, D_MODEL), jnp.float32),
                        pltpu.SemaphoreType.DMA((2,))])
    return pl.pallas_call(
        _combine_kernel,
        out_shape=jax.ShapeDtypeStruct((n, D_MODEL), jnp.float32),
        grid_spec=grid_spec,
        compiler_params=_cparams(1),
        name="combine",
    )(slot_km, x1, mod_l, rw, ys)


def _pad_lanes(v, width=LANES):
    return jnp.pad(v, [(0, 0)] * (v.ndim - 1) + [(0, width - v.shape[-1])])


def _constants(sample_len):
    r = jnp.arange(LANES)
    c512 = jnp.arange(SSD_WIDTH)
    ef = (r[:, None] == (c512[None, :] // SSD_HEAD_DIM)).astype(MXU_DTYPE)
    eb = (r[:, None] == (c512[None, :] // SSD_HEAD_DIM + SSD_HEADS)).astype(MXU_DTYPE)
    t = jnp.arange(SSD_CHUNK)
    tril = (t[None, :] <= t[:, None]).astype(MXU_DTYPE)
    triu = (t[None, :] >= t[:, None]).astype(MXU_DTYPE)
    qk = jnp.arange(COL_V)
    bd_qk = (qk[:, None] // HEAD_DIM == qk[None, :] // HEAD_DIM).astype(MXU_DTYPE)
    cw = jnp.arange(CONV_WIDTH)
    gsz = CONV_WIDTH // CONV_GROUPS
    bd_cm = (cw[:, None] // gsz == cw[None, :] // gsz).astype(MXU_DTYPE)
    pos = jnp.arange(sample_len)
    row = (pos // GRID_W).astype(jnp.float32)
    col = (pos % GRID_W).astype(jnp.float32)
    n_freq = HEAD_DIM // 4
    inv = ROPE_THETA ** (-jnp.arange(n_freq, dtype=jnp.float32) / n_freq)
    ar = row[:, None] * inv
    ac = col[:, None] * inv
    cos64 = jnp.concatenate([jnp.cos(ar), jnp.cos(ar), jnp.cos(ac), jnp.cos(ac)], axis=1)
    sin64 = jnp.concatenate([-jnp.sin(ar), jnp.sin(ar), -jnp.sin(ac), jnp.sin(ac)], axis=1)
    cos_t = jnp.concatenate([jnp.tile(cos64, (1, 2)), jnp.ones((TM_PROJ, LANES), jnp.float32)], axis=0)
    sin_t = jnp.concatenate([jnp.tile(sin64, (1, 2)), jnp.zeros((TM_PROJ, LANES), jnp.float32)], axis=0)
    return dict(ef=ef, eb=eb, tril=tril, triu=triu, bd_qk=bd_qk, bd_cm=bd_cm, cos=cos_t, sin=sin_t)


def _pack_w_in(w_in):
    o = 0
    parts = {}
    for name, size in (("q", ATTN_WIDTH), ("k", KV_WIDTH), ("v", KV_WIDTH), ("z", SSD_WIDTH),
                       ("xbc", SSD_XBC), ("dt", SSD_HEADS), ("glu", 2 * CONV_WIDTH)):
        parts[name] = w_in[:, o:o + size]
        o += size
    dt2 = _pad_lanes(jnp.concatenate([parts["dt"], parts["dt"]], axis=1))
    return _mx(jnp.concatenate([parts["q"], parts["k"], parts["v"], parts["z"], parts["xbc"],
                                parts["glu"], dt2], axis=1))


def _layer_params(l, norm1_w, norm2_w, w_in, q_norm_w, k_norm_w, attn_sink, ssd_conv_w, ssd_conv_b,
                  ssd_dt_bias, ssd_a_log, ssd_d, ssd_norm_w, cm_conv_w, cm_conv_b, cm_norm_w, cm_norm_b,
                  w_out, router_group_w, router_group_b, router_expert_w, router_expert_b):
    rw = _pad_lanes(jnp.concatenate([router_group_w[l], router_expert_w[l]], axis=1))
    rw_hi = _mx(rw)
    rw_lo = _mx(rw - _f32(rw_hi))
    return dict(
        norm1=norm1_w[l][None, :],
        norm2=norm2_w[l][None, :],
        w_in=_pack_w_in(w_in[l]),
        qkw=jnp.concatenate([jnp.tile(q_norm_w[l], ATTN_HEADS), jnp.tile(k_norm_w[l], KV_HEADS)])[None, :],
        sink=attn_sink[l],
        ssd_conv_w=ssd_conv_w[l],
        ssd_conv_b=ssd_conv_b[l][None, :],
        dtb=_pad_lanes(ssd_dt_bias[l].reshape(1, 2 * SSD_HEADS)),
        alog=_pad_lanes(ssd_a_log[l].reshape(1, 2 * SSD_HEADS)),
        dsum=jnp.repeat(ssd_d[l, 0] + ssd_d[l, 1], SSD_HEAD_DIM)[None, :],
        ssd_norm_w=ssd_norm_w[l][None, :],
        cm_conv_w=cm_conv_w[l],
        cm_conv_b=cm_conv_b[l][None, :],
        cm_norm_w=cm_norm_w[l][None, :],
        cm_norm_b=cm_norm_b[l][None, :],
        w_out=_mx(w_out[l]),
        rw_hi=rw_hi,
        rw_lo=rw_lo,
        rb=_pad_lanes(jnp.concatenate([router_group_b[l], router_expert_b[l]])[None, :]),
    )


def _chunk_tables(n_prompt_seq, prompt_len, n_sample_seq, sample_len):
    t = SSD_CHUNK
    pc, sc = prompt_len // t, sample_len // t
    seq = [s for s in range(n_prompt_seq) for _ in range(pc)]
    seq += [n_prompt_seq + s for s in range(n_sample_seq) for _ in range(sc)]
    first = [int(i == 0) for _ in range(n_prompt_seq) for i in range(pc)]
    first += [int(i == 0) for _ in range(n_sample_seq) for i in range(sc)]
    last = [int(i == pc - 1) for _ in range(n_prompt_seq) for i in range(pc)]
    last += [int(i == sc - 1) for _ in range(n_sample_seq) for i in range(sc)]
    return (jnp.asarray(seq, jnp.int32), jnp.asarray(first, jnp.int32), jnp.asarray(last, jnp.int32))


def _layer(x, mod_l, pr, consts, tables, ck, cv, h0, wg, wu, wd, layer, *, n_prompt, prompt_len, sample_len):
    n = x.shape[0]
    n_sample = n - n_prompt
    q, kr, kn, v, z, xbc, glu, dt = _in_projection(
        x, mod_l, pr["norm1"], pr["w_in"], pr["qkw"], consts["bd_qk"], consts["cos"], consts["sin"],
        n_prompt, sample_len)
    attn_p = _attention(pr["sink"], q, kr, v, (kr, 0), (v, 0), tok_off=0, n_tok=n_prompt,
                        seq_len=prompt_len, ctx_len=prompt_len, local=False)
    attn_s = _attention(pr["sink"], q, kr, v, (ck, 0), (cv, 0), tok_off=n_prompt, n_tok=n_sample,
                        seq_len=sample_len, ctx_len=ck.shape[0] // (n_sample // sample_len), local=True)
    attn = jnp.concatenate([attn_p, attn_s], axis=0)
    ydiag, st, cdec, cm, din, u = _ssd_local(xbc, glu, dt, pr, consts, n_prompt=n_prompt,
                                             prompt_len=prompt_len, sample_len=sample_len)
    hsf, hsb, hff, hfb = _ssd_scan(*tables, st, cdec, h0)
    x1, xm, ridx, rwt = _out_projection(x, attn, u, ydiag, z, cm, din, hsf, hsb, mod_l, pr, consts,
                                        n_prompt=n_prompt, sample_len=sample_len)
    slot, src, block_e, n_valid = _route_slots(ridx)
    ys = _experts(block_e, src, n_valid, xm, wg, wu, wd, layer)
    slot_km = slot.reshape(n // TM_COMB, TM_COMB, 2).transpose(0, 2, 1).reshape(-1)
    x2 = _combine(slot_km, x1, mod_l, rwt, ys, n_prompt=n_prompt, sample_len=sample_len)
    return x2, kn, v, hff, hfb


def kernel(x_prompt, x_sample, c, cache_k, cache_v, state_ssm, c_ctx, mod_w, mod_b, norm1_w, norm2_w, w_in, q_norm_w, k_norm_w, attn_sink, ssd_conv_w, ssd_conv_b, ssd_dt_bias, ssd_a_log, ssd_d, ssd_norm_w, cm_conv_w, cm_conv_b, cm_norm_w, cm_norm_b, w_out, router_group_w, router_group_b, router_expert_w, router_expert_b, expert_w_gate, expert_w_up, expert_w_down):
    bp, lp, d = x_prompt.shape
    bs, ls, _ = x_sample.shape
    depth = mod_w.shape[0]
    past = cache_k.shape[2]
    n_prompt = bp * lp
    assert d == D_MODEL and bs + 1 <= SUBLANES
    assert lp % TM_OUT == 0 and ls % TM_OUT == 0 and lp % TM_PROJ == 0 and ls % TM_PROJ == 0

    cond8 = jnp.concatenate([c_ctx[None, :], c, jnp.zeros((SUBLANES - 1 - bs, d), jnp.float32)], axis=0)
    mod = _modulation(cond8, mod_w, mod_b).reshape(depth, SUBLANES, 6, d)
    consts = _constants(ls)
    tables = _chunk_tables(bp, lp, bs, ls)

    x = jnp.concatenate([x_prompt.reshape(n_prompt, d), x_sample.reshape(bs * ls, d)], axis=0)
    ks, vs, sts = [], [], []
    for l in range(depth):
        pr = _layer_params(l, norm1_w, norm2_w, w_in, q_norm_w, k_norm_w, attn_sink, ssd_conv_w, ssd_conv_b,
                           ssd_dt_bias, ssd_a_log, ssd_d, ssd_norm_w, cm_conv_w, cm_conv_b, cm_norm_w,
                           cm_norm_b, w_out, router_group_w, router_group_b, router_expert_w,
                           router_expert_b)
        ck = cache_k[:, l].reshape(bs * past, KV_WIDTH)
        cv = cache_v[:, l].reshape(bs * past, KV_WIDTH)
        h0s = state_ssm[:, l].transpose(0, 1, 4, 2, 3).reshape(bs, 2, SSD_STATE, SSD_WIDTH)
        h0 = jnp.concatenate([jnp.zeros((bp, 2, SSD_STATE, SSD_WIDTH), jnp.float32), h0s], axis=0)
        x, kn, v, hff, hfb = _layer(x, mod[l], pr, consts, tables, ck, cv, h0,
                                    expert_w_gate, expert_w_up, expert_w_down, l,
                                    n_prompt=n_prompt, prompt_len=lp, sample_len=ls)
        ks.append(kn[:n_prompt].reshape(bp, lp, KV_HEADS, HEAD_DIM))
        vs.append(v[:n_prompt].reshape(bp, lp, KV_HEADS, HEAD_DIM))
        hfin = jnp.concatenate([hff[:bp], hfb[:bp]], axis=1)
        sts.append(hfin.reshape(bp, 2, SSD_STATE, SSD_HEADS, SSD_HEAD_DIM).transpose(0, 1, 3, 4, 2))
    y_prompt = x[:n_prompt].reshape(bp, lp, d)
    y_sample = x[n_prompt:].reshape(bs, ls, d)
    return (y_prompt, y_sample, jnp.stack(ks, axis=1), jnp.stack(vs, axis=1), jnp.stack(sts, axis=1))
```

```python
import functools
import math

import jax
import jax.numpy as jnp
from jax import lax
from jax.experimental import pallas as pl
from jax.experimental.pallas import tpu as pltpu

D_MODEL = 1024
GRID_W = 64
HEAD_DIM = 64
ATTN_HEADS = 4
KV_HEADS = 2
ATTN_WIDTH = ATTN_HEADS * HEAD_DIM
KV_WIDTH = KV_HEADS * HEAD_DIM
WINDOW = 128
ATTN_BLOCK = 128
ROPE_THETA = 10000.0
SSD_HEAD_DIM = 64
SSD_WIDTH = 512
SSD_HEADS = 8
SSD_GROUPS = 2
SSD_STATE = 64
SSD_CONV = 5
SSD_CHUNK = 128
SSD_XBC = SSD_WIDTH + 2 * SSD_GROUPS * SSD_STATE
CONV_WIDTH = 256
CONV_KERNEL = 31
CONV_GROUPS = 4
N_EXPERT_GROUPS = 4
EXPERTS_PER_GROUP = 8
N_EXPERTS = 32
EXPERT_FF = 256
EPS = 1e-6

LANES = 128
SUBLANES = 8
VMEM_LIMIT = 52 * 1024 * 1024

MXU_DTYPE = jnp.bfloat16
TM_PROJ = 256
TM_OUT = 2 * SSD_CHUNK
MOE_BLK = 256
TM_COMB = 256
MOD_TN = 1536
HALO_SSD = 8
HALO_CM = 16
ROW_TILES = D_MODEL // LANES
GATHER_UNROLL = 8
COL_Q = 0
COL_K = ATTN_WIDTH
COL_V = COL_K + KV_WIDTH
COL_Z = COL_V + KV_WIDTH
COL_XBC = COL_Z + SSD_WIDTH
COL_GLU = COL_XBC + SSD_XBC
COL_DT = COL_GLU + 2 * CONV_WIDTH
PROJ_COLS = COL_DT + LANES


def _f32(x):
    return x.astype(jnp.float32)


def _mx(x):
    return x.astype(MXU_DTYPE)


def _dot(a, b):
    return jnp.dot(a, b, preferred_element_type=jnp.float32)


def _dot_nt(a, b):
    return lax.dot_general(a, b, (((1,), (1,)), ((), ())), preferred_element_type=jnp.float32)


def _split2(x):
    hi = _mx(x)
    lo = _mx(x - _f32(hi))
    return hi, lo


def _split3(x):
    hi = _mx(x)
    r = x - _f32(hi)
    mid = _mx(r)
    lo = _mx(r - _f32(mid))
    return hi, mid, lo


def _dot_sel2(x, sel):
    hi, lo = _split2(x)
    return _dot(hi, sel) + _dot(lo, sel)


def _sel_dot3(sel, x):
    hi, mid, lo = _split3(x)
    return _dot(sel, hi) + _dot(sel, mid) + _dot(sel, lo)


def _silu(x):
    return x * jax.nn.sigmoid(x)


def _softplus(x):
    return jnp.maximum(x, 0.0) + jnp.log(1.0 + jnp.exp(-jnp.abs(x)))


def _rows_from_tiles(ref):
    return jnp.concatenate([ref[:, s, :] for s in range(ROW_TILES)], axis=1)


def _rows_to_tiles(ref, val):
    for s in range(ROW_TILES):
        ref[:, s, :] = val[:, s * LANES:(s + 1) * LANES]


def _cparams(n_axes=1):
    return pltpu.CompilerParams(dimension_semantics=("arbitrary",) * n_axes,
                                vmem_limit_bytes=VMEM_LIMIT)


def _full(shape):
    nd = len(shape)
    return pl.BlockSpec(shape, lambda *_: (0,) * nd)


def _mod_kernel(c_ref, w_ref, b_ref, o_ref):
    s = _silu(c_ref[...])
    o_ref[0] = _dot(_mx(s), _mx(w_ref[0])) + b_ref[0]


def _modulation(cond8, mod_w, mod_b):
    depth, d, n6 = mod_w.shape
    return pl.pallas_call(
        _mod_kernel,
        out_shape=jax.ShapeDtypeStruct((depth, SUBLANES, n6), jnp.float32),
        grid=(depth, n6 // MOD_TN),
        in_specs=[pl.BlockSpec((SUBLANES, d), lambda l, j: (0, 0)),
                  pl.BlockSpec((1, d, MOD_TN), lambda l, j: (l, 0, j)),
                  pl.BlockSpec((1, 1, MOD_TN), lambda l, j: (l, 0, j))],
        out_specs=pl.BlockSpec((1, SUBLANES, MOD_TN), lambda l, j: (l, 0, j)),
        compiler_params=_cparams(2),
        name="modulation",
    )(cond8, mod_w, mod_b.reshape(depth, 1, n6))


def _rope128(x, cos, sin, first_half):
    partner = jnp.where(first_half, pltpu.roll(x, LANES - 16, 1), pltpu.roll(x, 16, 1))
    return x * cos + partner * sin


def _inproj_kernel(x_ref, m_ref, n1_ref, w_ref, qkw_ref, bd_ref, cos_ref, sin_ref,
                   q_ref, kr_ref, kn_ref, v_ref, z_ref, xbc_ref, glu_ref, dt_ref):
    x = x_ref[...]
    xn = x * lax.rsqrt(jnp.mean(x * x, axis=-1, keepdims=True) + EPS) * n1_ref[...]
    h = xn * (1.0 + m_ref[0, 1:2, :]) + m_ref[0, 0:1, :]
    p = _dot(_mx(h), w_ref[...])
    qk = p[:, COL_Q:COL_V]
    seg = _dot_sel2(qk * qk, bd_ref[...])
    qk = qk * lax.rsqrt(seg * (1.0 / HEAD_DIM) + EPS) * qkw_ref[...]
    cos = cos_ref[...]
    sin = sin_ref[...]
    lane = lax.broadcasted_iota(jnp.int32, cos.shape, 1)
    first_half = (lane % 32) < 16
    scale = HEAD_DIM ** -0.5
    for c in range(ATTN_WIDTH // LANES):
        qc = _rope128(qk[:, c * LANES:(c + 1) * LANES], cos, sin, first_half)
        q_ref[:, c * LANES:(c + 1) * LANES] = (qc * scale).astype(q_ref.dtype)
    kn = qk[:, COL_K:COL_V]
    kn_ref[...] = kn
    kr_ref[...] = _rope128(kn, cos, sin, first_half).astype(kr_ref.dtype)
    v_ref[...] = p[:, COL_V:COL_Z]
    z_ref[...] = p[:, COL_Z:COL_XBC]
    xbc_ref[...] = p[:, COL_XBC:COL_GLU]
    glu_ref[...] = p[:, COL_GLU:COL_DT]
    dt_ref[...] = p[:, COL_DT:PROJ_COLS]


def _in_projection(x, mod_l, n1, w_packed, qkw, bd, cos_t, sin_t, n_prompt, sample_len):
    n = x.shape[0]
    tm = TM_PROJ
    ident_blk = sample_len // tm

    def cond_row(t):
        s = t * tm
        return jnp.where(s < n_prompt, 0, 1 + (s - n_prompt) // sample_len)

    def rope_blk(t):
        s = t * tm
        return jnp.where(s < n_prompt, ident_blk, ((s - n_prompt) % sample_len) // tm)

    def rows(width):
        return pl.BlockSpec((tm, width), lambda t: (t, 0))

    widths = (ATTN_WIDTH, KV_WIDTH, KV_WIDTH, KV_WIDTH, SSD_WIDTH, SSD_XBC, 2 * CONV_WIDTH, LANES)
    dtypes = (MXU_DTYPE, MXU_DTYPE, jnp.float32, jnp.float32, jnp.float32, jnp.float32, jnp.float32,
              jnp.float32)
    return pl.pallas_call(
        _inproj_kernel,
        out_shape=[jax.ShapeDtypeStruct((n, w), dt) for w, dt in zip(widths, dtypes)],
        grid=(n // tm,),
        in_specs=[rows(D_MODEL),
                  pl.BlockSpec((1, 6, D_MODEL), lambda t: (cond_row(t), 0, 0)),
                  _full((1, D_MODEL)),
                  _full((D_MODEL, PROJ_COLS)),
                  _full((1, COL_V)),
                  _full((COL_V, COL_V)),
                  pl.BlockSpec((tm, LANES), lambda t: (rope_blk(t), 0)),
                  pl.BlockSpec((tm, LANES), lambda t: (rope_blk(t), 0))],
        out_specs=[rows(w) for w in widths],
        compiler_params=_cparams(1),
        name="in_projection",
    )(x, mod_l, n1, w_packed, qkw, bd, cos_t, sin_t)


def _attn_kernel(sink_ref, q_ref, kc_ref, vc_ref, *rest, local, blocks_per_seq):
    if local:
        kp_ref, kcur_ref, knx_ref, vp_ref, vcur_ref, vnx_ref, o_ref = rest
    else:
        (o_ref,) = rest
    blk = ATTN_BLOCK
    j = pl.program_id(0) % blocks_per_seq
    q = q_ref[...]
    row = lax.broadcasted_iota(jnp.int32, (2 * blk, 1), 0)
    if local:
        qo = lax.broadcasted_iota(jnp.int32, (2 * blk, 3 * blk), 0) % blk
        kpos = lax.broadcasted_iota(jnp.int32, (2 * blk, 3 * blk), 1) - blk
        pos = j * blk + kpos
        valid = (jnp.abs(kpos - qo) <= WINDOW) & (pos >= 0) & (pos < blocks_per_seq * blk)
    for g in range(KV_HEADS):
        sl = slice(g * HEAD_DIM, (g + 1) * HEAD_DIM)
        q2 = jnp.concatenate([q[:, (2 * g) * HEAD_DIM:(2 * g + 1) * HEAD_DIM],
                              q[:, (2 * g + 1) * HEAD_DIM:(2 * g + 2) * HEAD_DIM]], axis=0)
        s_ctx = _dot_nt(q2, _mx(kc_ref[:, sl]))
        sink = jnp.where(row < blk, sink_ref[2 * g], sink_ref[2 * g + 1])
        m = jnp.maximum(jnp.max(s_ctx, axis=-1, keepdims=True), sink)
        if local:
            kl = jnp.concatenate([_mx(kp_ref[:, sl]), _mx(kcur_ref[:, sl]), _mx(knx_ref[:, sl])], axis=0)
            vl = jnp.concatenate([_mx(vp_ref[:, sl]), _mx(vcur_ref[:, sl]), _mx(vnx_ref[:, sl])], axis=0)
            s_loc = jnp.where(valid, _dot_nt(q2, kl), -jnp.inf)
            m = jnp.maximum(m, jnp.max(s_loc, axis=-1, keepdims=True))
        p_ctx = jnp.exp(s_ctx - m)
        den = jnp.sum(p_ctx, axis=-1, keepdims=True) + jnp.exp(sink - m)
        o = _dot(_mx(p_ctx), _mx(vc_ref[:, sl]))
        if local:
            p_loc = jnp.exp(s_loc - m)
            den = den + jnp.sum(p_loc, axis=-1, keepdims=True)
            o = o + _dot(_mx(p_loc), vl)
        o = o / den
        o_ref[:, (2 * g) * HEAD_DIM:(2 * g + 1) * HEAD_DIM] = o[:blk]
        o_ref[:, (2 * g + 1) * HEAD_DIM:(2 * g + 2) * HEAD_DIM] = o[blk:]


def _attention(sink, q, k, v, kctx, vctx, *, tok_off, n_tok, seq_len, ctx_len, local):
    blk = ATTN_BLOCK
    bps = seq_len // blk
    off = tok_off // blk
    nb = n_tok // blk
    kctx_arr, kctx_off = kctx
    vctx_arr, vctx_off = vctx
    cb_k = kctx_off // ctx_len
    cb_v = vctx_off // ctx_len

    def prev_blk(i):
        return off + (i // bps) * bps + jnp.maximum(i % bps - 1, 0)

    def next_blk(i):
        return off + (i // bps) * bps + jnp.minimum(i % bps + 1, bps - 1)

    in_specs = [pl.BlockSpec(memory_space=pltpu.SMEM),
                pl.BlockSpec((blk, ATTN_WIDTH), lambda i: (off + i, 0)),
                pl.BlockSpec((ctx_len, KV_WIDTH), lambda i: (cb_k + i // bps, 0)),
                pl.BlockSpec((ctx_len, KV_WIDTH), lambda i: (cb_v + i // bps, 0))]
    args = [sink, q, kctx_arr, vctx_arr]
    if local:
        for arr in (k, v):
            in_specs += [pl.BlockSpec((blk, KV_WIDTH), lambda i: (prev_blk(i), 0)),
                         pl.BlockSpec((blk, KV_WIDTH), lambda i: (off + i, 0)),
                         pl.BlockSpec((blk, KV_WIDTH), lambda i: (next_blk(i), 0))]
            args += [arr, arr, arr]
    return pl.pallas_call(
        functools.partial(_attn_kernel, local=local, blocks_per_seq=bps),
        out_shape=jax.ShapeDtypeStruct((n_tok, ATTN_WIDTH), jnp.float32),
        grid=(nb,),
        in_specs=in_specs,
        out_specs=pl.BlockSpec((blk, ATTN_WIDTH), lambda i: (i, 0)),
        compiler_params=_cparams(1),
        name="attention_local" if local else "attention_ctx",
    )(*args)


def _ssd_local_kernel(xbc_ref, xp_ref, xn_ref, glu_ref, gp_ref, gn_ref, dt_ref,
                      cw_ref, cb_ref, dtb_ref, alog_ref, dsum_ref, ef_ref, eb_ref, tril_ref, triu_ref,
                      mw_ref, mb_ref, nw_ref, nb_ref, bd_ref,
                      y_ref, st_ref, cdec_ref, cm_ref, din_ref, u_ref,
                      xext, gext, *, n_prompt_chunks, prompt_cps, sample_cps):
    t = SSD_CHUNK
    g = pl.program_id(0)
    in_prompt = g < n_prompt_chunks
    cps = jnp.where(in_prompt, prompt_cps, sample_cps)
    cidx = jnp.where(in_prompt, g % prompt_cps, (g - n_prompt_chunks) % sample_cps)
    not_first = (cidx > 0).astype(jnp.float32)
    not_last = (cidx < cps - 1).astype(jnp.float32)

    xext[0:HALO_SSD, :] = xp_ref[...] * not_first
    xext[HALO_SSD:HALO_SSD + t, :] = xbc_ref[...]
    xext[HALO_SSD + t:HALO_SSD + t + HALO_SSD, :] = xn_ref[...] * not_last
    half = (SSD_CONV - 1) // 2
    acc = jnp.zeros((t, SSD_XBC), jnp.float32) + cb_ref[...]
    for k in range(SSD_CONV):
        acc = acc + xext[pl.ds(HALO_SSD - half + k, t), :] * cw_ref[k:k + 1, :]
    xc = _silu(acc)
    xs = xc[:, :SSD_WIDTH]
    bm = xc[:, SSD_WIDTH:SSD_WIDTH + SSD_GROUPS * SSD_STATE]
    cm = xc[:, SSD_WIDTH + SSD_GROUPS * SSD_STATE:]
    cm_ref[...] = cm

    lane = lax.broadcasted_iota(jnp.int32, (t, LANES), 1)
    fwd_lane = lane < SSD_HEADS
    dt = _softplus(dt_ref[...] + dtb_ref[...])
    la = dt * (-jnp.exp(alog_ref[...]))
    cs = jnp.where(fwd_lane, _sel_dot3(tril_ref[...], la), _sel_dot3(triu_ref[...], la))
    tot = jnp.where(fwd_lane[0:1], cs[t - 1:t, :], cs[0:1, :])
    din = jnp.exp(cs)
    din_ref[...] = din
    dec = jnp.exp(tot - cs)
    cs_row = cs.T
    ef = ef_ref[...]
    eb = eb_ref[...]
    cd = jnp.broadcast_to(jnp.exp(tot), (SUBLANES, LANES))
    cdec_ref[0] = jnp.concatenate([_dot_sel2(cd, ef)[0:1], _dot_sel2(cd, eb)[0:1],
                                   jnp.zeros((SUBLANES - 2, SSD_WIDTH), jnp.float32)], axis=0)

    bm_t = bm.T
    li = lax.broadcasted_iota(jnp.int32, (t, t), 0)
    si = lax.broadcasted_iota(jnp.int32, (t, t), 1)
    scores = [_dot_nt(_mx(cm[:, gi * SSD_STATE:(gi + 1) * SSD_STATE]),
                      _mx(bm[:, gi * SSD_STATE:(gi + 1) * SSD_STATE])) for gi in range(SSD_GROUPS)]
    hpg = SSD_HEADS // SSD_GROUPS
    gw = hpg * SSD_HEAD_DIM
    y_heads = [None] * SSD_HEADS
    for d, e_d in enumerate((ef, eb)):
        xdt = xs * _dot_sel2(dt, e_d)
        xdec = _mx(xdt * _dot_sel2(dec, e_d))
        st_ref[0, d] = jnp.concatenate(
            [_dot(_mx(bm_t[gi * SSD_STATE:(gi + 1) * SSD_STATE, :]), xdec[:, gi * gw:(gi + 1) * gw])
             for gi in range(SSD_GROUPS)], axis=1)
        mask = (si <= li) if d == 0 else (si >= li)
        for h in range(SSD_HEADS):
            c = d * SSD_HEADS + h
            seg = cs[:, c:c + 1] - cs_row[c:c + 1, :]
            lmat = jnp.exp(jnp.where(mask, seg, -jnp.inf))
            yh = _dot(_mx(scores[h // hpg] * lmat), _mx(xdt[:, h * SSD_HEAD_DIM:(h + 1) * SSD_HEAD_DIM]))
            y_heads[h] = yh if y_heads[h] is None else y_heads[h] + yh
    y_ref[...] = jnp.concatenate(y_heads, axis=1) + xs * dsum_ref[...]

    def glu(v):
        return v[:, :CONV_WIDTH] * jax.nn.sigmoid(v[:, CONV_WIDTH:])

    gext[0:HALO_CM, :] = glu(gp_ref[...]) * not_first
    gext[HALO_CM:HALO_CM + t, :] = glu(glu_ref[...])
    gext[HALO_CM + t:HALO_CM + t + HALO_CM, :] = glu(gn_ref[...]) * not_last
    halfc = (CONV_KERNEL - 1) // 2
    u = jnp.zeros((t, CONV_WIDTH), jnp.float32) + mb_ref[...]
    for k in range(CONV_KERNEL):
        u = u + gext[pl.ds(HALO_CM - halfc + k, t), :] * mw_ref[k:k + 1, :]
    gsz = CONV_WIDTH // CONV_GROUPS
    bd = bd_ref[...]
    uc = u - _dot_sel2(u, bd) * (1.0 / gsz)
    un = uc * lax.rsqrt(_dot_sel2(uc * uc, bd) * (1.0 / gsz) + EPS)
    u_ref[...] = _silu(un * nw_ref[...] + nb_ref[...])


def _ssd_local(xbc, glu, dt, pr, consts, *, n_prompt, prompt_len, sample_len):
    n = xbc.shape[0]
    t = SSD_CHUNK
    nch = n // t
    r8 = t // HALO_SSD
    r16 = t // HALO_CM
    kern = functools.partial(_ssd_local_kernel, n_prompt_chunks=n_prompt // t,
                             prompt_cps=prompt_len // t, sample_cps=sample_len // t)
    out_shape = [jax.ShapeDtypeStruct((n, SSD_WIDTH), jnp.float32),
                 jax.ShapeDtypeStruct((nch, 2, SSD_STATE, SSD_WIDTH), jnp.float32),
                 jax.ShapeDtypeStruct((nch, SUBLANES, SSD_WIDTH), jnp.float32),
                 jax.ShapeDtypeStruct((n, LANES), jnp.float32),
                 jax.ShapeDtypeStruct((n, LANES), jnp.float32),
                 jax.ShapeDtypeStruct((n, CONV_WIDTH), jnp.float32)]
    out_specs = [pl.BlockSpec((t, SSD_WIDTH), lambda g: (g, 0)),
                 pl.BlockSpec((1, 2, SSD_STATE, SSD_WIDTH), lambda g: (g, 0, 0, 0)),
                 pl.BlockSpec((1, SUBLANES, SSD_WIDTH), lambda g: (g, 0, 0)),
                 pl.BlockSpec((t, LANES), lambda g: (g, 0)),
                 pl.BlockSpec((t, LANES), lambda g: (g, 0)),
                 pl.BlockSpec((t, CONV_WIDTH), lambda g: (g, 0))]
    in_specs = [pl.BlockSpec((t, SSD_XBC), lambda g: (g, 0)),
                pl.BlockSpec((HALO_SSD, SSD_XBC), lambda g: (jnp.maximum(g * r8 - 1, 0), 0)),
                pl.BlockSpec((HALO_SSD, SSD_XBC), lambda g: (jnp.minimum((g + 1) * r8, n // HALO_SSD - 1), 0)),
                pl.BlockSpec((t, 2 * CONV_WIDTH), lambda g: (g, 0)),
                pl.BlockSpec((HALO_CM, 2 * CONV_WIDTH), lambda g: (jnp.maximum(g * r16 - 1, 0), 0)),
                pl.BlockSpec((HALO_CM, 2 * CONV_WIDTH),
                             lambda g: (jnp.minimum((g + 1) * r16, n // HALO_CM - 1), 0)),
                pl.BlockSpec((t, LANES), lambda g: (g, 0))]
    params = [pr["ssd_conv_w"], pr["ssd_conv_b"], pr["dtb"], pr["alog"], pr["dsum"],
              consts["ef"], consts["eb"], consts["tril"], consts["triu"],
              pr["cm_conv_w"], pr["cm_conv_b"], pr["cm_norm_w"], pr["cm_norm_b"], consts["bd_cm"]]
    in_specs += [_full(p.shape) for p in params]
    return pl.pallas_call(
        kern,
        out_shape=out_shape,
        grid=(nch,),
        in_specs=in_specs,
        out_specs=out_specs,
        scratch_shapes=[pltpu.VMEM((t + 2 * HALO_SSD, SSD_XBC), jnp.float32),
                        pltpu.VMEM((t + 2 * HALO_CM, CONV_WIDTH), jnp.float32)],
        compiler_params=_cparams(1),
        name="ssd_local",
    )(xbc, xbc, xbc, glu, glu, glu, dt, *params)


def _ssd_scan_kernel(seq_ref, first_ref, last_ref, stf_ref, stb_ref, cdf_ref, cdb_ref, h0f_ref, h0b_ref,
                     hsf_ref, hsb_ref, hff_ref, hfb_ref, hf, hb):
    g = pl.program_id(0)
    gb = pl.num_programs(0) - 1 - g

    @pl.when(first_ref[g] == 1)
    def _():
        hf[...] = h0f_ref[0, 0]

    @pl.when(last_ref[gb] == 1)
    def _():
        hb[...] = h0b_ref[0, 0]

    hsf_ref[0] = hf[...]
    hf_new = hf[...] * cdf_ref[0, 0:1, :] + stf_ref[0, 0]
    hf[...] = hf_new
    hff_ref[0, 0] = hf_new
    hsb_ref[0] = hb[...]
    hb_new = hb[...] * cdb_ref[0, 1:2, :] + stb_ref[0, 0]
    hb[...] = hb_new
    hfb_ref[0, 0] = hb_new


def _ssd_scan(seq_id, first, last, st, cdec, h0):
    nch = st.shape[0]
    nseq = h0.shape[0]
    blk4 = (1, 1, SSD_STATE, SSD_WIDTH)
    grid_spec = pltpu.PrefetchScalarGridSpec(
        num_scalar_prefetch=3,
        grid=(nch,),
        in_specs=[pl.BlockSpec(blk4, lambda g, s, f, l: (g, 0, 0, 0)),
                  pl.BlockSpec(blk4, lambda g, s, f, l: (nch - 1 - g, 1, 0, 0)),
                  pl.BlockSpec((1, SUBLANES, SSD_WIDTH), lambda g, s, f, l: (g, 0, 0)),
                  pl.BlockSpec((1, SUBLANES, SSD_WIDTH), lambda g, s, f, l: (nch - 1 - g, 0, 0)),
                  pl.BlockSpec(blk4, lambda g, s, f, l: (s[g], 0, 0, 0)),
                  pl.BlockSpec(blk4, lambda g, s, f, l: (s[nch - 1 - g], 1, 0, 0))],
        out_specs=[pl.BlockSpec((1, SSD_STATE, SSD_WIDTH), lambda g, s, f, l: (g, 0, 0)),
                   pl.BlockSpec((1, SSD_STATE, SSD_WIDTH), lambda g, s, f, l: (nch - 1 - g, 0, 0)),
                   pl.BlockSpec(blk4, lambda g, s, f, l: (s[g], 0, 0, 0)),
                   pl.BlockSpec(blk4, lambda g, s, f, l: (s[nch - 1 - g], 0, 0, 0))],
        scratch_shapes=[pltpu.VMEM((SSD_STATE, SSD_WIDTH), jnp.float32),
                        pltpu.VMEM((SSD_STATE, SSD_WIDTH), jnp.float32)])
    return pl.pallas_call(
        _ssd_scan_kernel,
        out_shape=[jax.ShapeDtypeStruct((nch, SSD_STATE, SSD_WIDTH), jnp.float32),
                   jax.ShapeDtypeStruct((nch, SSD_STATE, SSD_WIDTH), jnp.float32),
                   jax.ShapeDtypeStruct((nseq, 1, SSD_STATE, SSD_WIDTH), jnp.float32),
                   jax.ShapeDtypeStruct((nseq, 1, SSD_STATE, SSD_WIDTH), jnp.float32)],
        grid_spec=grid_spec,
        compiler_params=_cparams(1),
        name="ssd_scan",
    )(seq_id, first, last, st, st, cdec, cdec, h0, h0)


def _outproj_kernel(x_ref, attn_ref, u_ref, y_ref, z_ref, cm_ref, din_ref, hsf_ref, hsb_ref, m_ref,
                    sw_ref, wo_ref, n2_ref, rwh_ref, rwl_ref, rb_ref, ef_ref, eb_ref,
                    x1_ref, xm_ref, ri_ref, rw_ref):
    t = SSD_CHUNK
    hpg = SSD_HEADS // SSD_GROUPS
    gw = hpg * SSD_HEAD_DIM
    ys = []
    for c in range(TM_OUT // t):
        rs = slice(c * t, (c + 1) * t)
        cmc = cm_ref[rs, :]
        dinc = din_ref[rs, :]
        y = y_ref[rs, :]
        for hs_ref, e_ref in ((hsf_ref, ef_ref), (hsb_ref, eb_ref)):
            hs = hs_ref[c]
            yoff = jnp.concatenate(
                [_dot(_mx(cmc[:, gi * SSD_STATE:(gi + 1) * SSD_STATE]), _mx(hs[:, gi * gw:(gi + 1) * gw]))
                 for gi in range(SSD_GROUPS)], axis=1)
            y = y + yoff * _dot_sel2(dinc, e_ref[...])
        ys.append(y)
    y = jnp.concatenate(ys, axis=0) * _silu(z_ref[...])
    nrm = []
    for gi in range(SSD_GROUPS):
        yg = y[:, gi * gw:(gi + 1) * gw]
        nrm.append(yg * lax.rsqrt(jnp.mean(yg * yg, axis=-1, keepdims=True) + EPS))
    y = jnp.concatenate(nrm, axis=1) * sw_ref[...]
    o1 = ATTN_WIDTH
    o2 = ATTN_WIDTH + SSD_WIDTH
    mix = (_dot(_mx(attn_ref[...]), wo_ref[0:o1, :]) + _dot(_mx(y), wo_ref[o1:o2, :])
           + _dot(_mx(u_ref[...]), wo_ref[o2:, :]))
    x1 = x_ref[...] + m_ref[0, 2:3, :] * mix
    x1_ref[...] = x1
    xm = (x1 * lax.rsqrt(jnp.mean(x1 * x1, axis=-1, keepdims=True) + EPS) * n2_ref[...]
          * (1.0 + m_ref[0, 4:5, :]) + m_ref[0, 3:4, :])
    _rows_to_tiles(xm_ref, xm)
    xh, xl = _split2(xm)
    lg = _dot(xh, rwh_ref[...]) + _dot(xh, rwl_ref[...]) + _dot(xl, rwh_ref[...]) + rb_ref[...]
    lane_i = lax.broadcasted_iota(jnp.int32, lg.shape, 1)
    lane = lane_i.astype(jnp.float32)
    ninf = -jnp.inf
    big = float(LANES)
    is_g = lane_i < N_EXPERT_GROUPS
    gmax = jnp.max(jnp.where(is_g, lg, ninf), axis=-1, keepdims=True)
    gidx = jnp.min(jnp.where(is_g & (lg == gmax), lane, big), axis=-1, keepdims=True)
    p_grp = 1.0 / jnp.sum(jnp.where(is_g, jnp.exp(lg - gmax), 0.0), axis=-1, keepdims=True)
    lo = N_EXPERT_GROUPS + EXPERTS_PER_GROUP * gidx
    in_e = (lane >= lo) & (lane < lo + EXPERTS_PER_GROUP)
    v1 = jnp.max(jnp.where(in_e, lg, ninf), axis=-1, keepdims=True)
    i1 = jnp.min(jnp.where(in_e & (lg == v1), lane, big), axis=-1, keepdims=True)
    rest = in_e & (lane != i1)
    v2 = jnp.max(jnp.where(rest, lg, ninf), axis=-1, keepdims=True)
    i2 = jnp.min(jnp.where(rest & (lg == v2), lane, big), axis=-1, keepdims=True)
    e2 = jnp.exp(v2 - v1)
    w1 = p_grp * (1.0 / (1.0 + e2))
    w2 = p_grp * (e2 / (1.0 + e2))
    e_first = (i1 - N_EXPERT_GROUPS).astype(jnp.int32)
    e_second = (i2 - N_EXPERT_GROUPS).astype(jnp.int32)
    ri_ref[...] = jnp.where(lane_i == 0, e_first, jnp.where(lane_i == 1, e_second, 0))
    rw_ref[...] = jnp.where(lane_i == 0, w1, jnp.where(lane_i == 1, w2, 0.0))


def _out_projection(x, attn, u, ydiag, z, cm, din, hsf, hsb, mod_l, pr, consts, *, n_prompt, sample_len):
    n = x.shape[0]
    tm = TM_OUT
    cpt = tm // SSD_CHUNK

    def cond_row(t):
        s = t * tm
        return jnp.where(s < n_prompt, 0, 1 + (s - n_prompt) // sample_len)

    def rows(width):
        return pl.BlockSpec((tm, width), lambda t: (t, 0))

    params = [pr["ssd_norm_w"], pr["w_out"], pr["norm2"], pr["rw_hi"], pr["rw_lo"], pr["rb"],
              consts["ef"], consts["eb"]]
    return pl.pallas_call(
        _outproj_kernel,
        out_shape=[jax.ShapeDtypeStruct((n, D_MODEL), jnp.float32),
                   jax.ShapeDtypeStruct((n, ROW_TILES, LANES), jnp.float32),
                   jax.ShapeDtypeStruct((n, LANES), jnp.int32),
                   jax.ShapeDtypeStruct((n, LANES), jnp.float32)],
        grid=(n // tm,),
        in_specs=[rows(D_MODEL), rows(ATTN_WIDTH), rows(CONV_WIDTH), rows(SSD_WIDTH), rows(SSD_WIDTH),
                  rows(LANES), rows(LANES),
                  pl.BlockSpec((cpt, SSD_STATE, SSD_WIDTH), lambda t: (t, 0, 0)),
                  pl.BlockSpec((cpt, SSD_STATE, SSD_WIDTH), lambda t: (t, 0, 0)),
                  pl.BlockSpec((1, 6, D_MODEL), lambda t: (cond_row(t), 0, 0))]
        + [_full(p.shape) for p in params],
        out_specs=[rows(D_MODEL), pl.BlockSpec((tm, ROW_TILES, LANES), lambda t: (t, 0, 0)),
                   rows(LANES), rows(LANES)],
        compiler_params=_cparams(1),
        name="out_projection",
    )(x, attn, u, ydiag, z, cm, din, hsf, hsb, mod_l, *params)


def _route_slots(ridx):
    n = ridx.shape[0]
    n_assign = 2 * n
    blk = MOE_BLK
    flat = ridx[:, :2].reshape(-1)
    onehot = (flat[:, None] == jnp.arange(N_EXPERTS, dtype=jnp.int32)[None, :]).astype(jnp.int32)
    csum = jnp.cumsum(onehot, axis=0)
    counts = csum[-1]
    padded = (counts + blk - 1) // blk * blk
    pad_end = jnp.cumsum(padded)
    pad_start = pad_end - padded
    slot = jnp.sum(onehot * (csum - 1 + pad_start[None, :]), axis=1).astype(jnp.int32)
    n_blocks = n_assign // blk + N_EXPERTS
    src = jnp.zeros((n_blocks * blk,), jnp.int32).at[slot].set(jnp.arange(n_assign, dtype=jnp.int32) // 2)
    starts = jnp.arange(n_blocks, dtype=jnp.int32) * blk
    block_e = jnp.minimum(jnp.sum((pad_end[None, :] <= starts[:, None]).astype(jnp.int32), axis=1),
                          N_EXPERTS - 1).astype(jnp.int32)
    n_valid = (pad_end[-1] // blk).astype(jnp.int32).reshape(1)
    return slot, src, block_e, n_valid


def _gather_rows(idx_ref, base, src_hbm, dst, sem, n_rows):
    def body(j, carry):
        for u in range(GATHER_UNROLL):
            jj = j * GATHER_UNROLL + u
            pltpu.make_async_copy(src_hbm.at[idx_ref[base + jj]], dst.at[jj], sem).start()
        return carry

    lax.fori_loop(0, n_rows // GATHER_UNROLL, body, 0)


def _wait_rows(src_hbm, dst, sem, n_rows):
    pltpu.make_async_copy(src_hbm.at[pl.ds(0, n_rows)], dst, sem).wait()


def _experts_kernel(be_ref, src_ref, nv_ref, x_hbm, wg_ref, wu_ref, wd_ref, o_ref,
                    xbuf, sem, wg_c, wu_c, wd_c):
    b = pl.program_id(0)
    nv = nv_ref[0]
    blk = MOE_BLK

    @pl.when(b == 0)
    def _():
        _gather_rows(src_ref, 0, x_hbm, xbuf.at[0], sem.at[0], blk)

    @pl.when(b + 1 < nv)
    def _():
        nxt = (b + 1) % 2
        _gather_rows(src_ref, (b + 1) * blk, x_hbm, xbuf.at[nxt], sem.at[nxt], blk)

    @pl.when((b == 0) | (be_ref[b] != be_ref[jnp.maximum(b - 1, 0)]))
    def _():
        wg_c[...] = _mx(wg_ref[0, 0])
        wu_c[...] = _mx(wu_ref[0, 0])
        wd_c[...] = _mx(wd_ref[0, 0])

    @pl.when(b < nv)
    def _():
        cur = b % 2
        _wait_rows(x_hbm, xbuf.at[cur], sem.at[cur], blk)
        xb = _mx(_rows_from_tiles(xbuf.at[cur]))
        hid = _silu(_dot(xb, wg_c[...])) * _dot(xb, wu_c[...])
        _rows_to_tiles(o_ref, _dot(_mx(hid), wd_c[...]))

    @pl.when(b >= nv)
    def _():
        o_ref[...] = jnp.zeros_like(o_ref)


def _experts(block_e, src, n_valid, xm, wg, wu, wd, layer):
    blk = MOE_BLK
    n_blocks = block_e.shape[0]
    grid_spec = pltpu.PrefetchScalarGridSpec(
        num_scalar_prefetch=3,
        grid=(n_blocks,),
        in_specs=[pl.BlockSpec(memory_space=pl.ANY),
                  pl.BlockSpec((1, 1, D_MODEL, EXPERT_FF), lambda b, be, s, nv: (layer, be[b], 0, 0)),
                  pl.BlockSpec((1, 1, D_MODEL, EXPERT_FF), lambda b, be, s, nv: (layer, be[b], 0, 0)),
                  pl.BlockSpec((1, 1, EXPERT_FF, D_MODEL), lambda b, be, s, nv: (layer, be[b], 0, 0))],
        out_specs=pl.BlockSpec((blk, ROW_TILES, LANES), lambda b, be, s, nv: (b, 0, 0)),
        scratch_shapes=[pltpu.VMEM((2, blk, ROW_TILES, LANES), jnp.float32),
                        pltpu.SemaphoreType.DMA((2,)),
                        pltpu.VMEM((D_MODEL, EXPERT_FF), MXU_DTYPE),
                        pltpu.VMEM((D_MODEL, EXPERT_FF), MXU_DTYPE),
                        pltpu.VMEM((EXPERT_FF, D_MODEL), MXU_DTYPE)])
    return pl.pallas_call(
        _experts_kernel,
        out_shape=jax.ShapeDtypeStruct((n_blocks * blk, ROW_TILES, LANES), jnp.float32),
        grid_spec=grid_spec,
        compiler_params=_cparams(1),
        name="experts",
    )(block_e, src, n_valid, xm, wg, wu, wd)


def _combine_kernel(slot_ref, x1_ref, m_ref, rw_ref, ys_hbm, o_ref, ybuf, sem):
    i = pl.program_id(0)
    n_steps = pl.num_programs(0)
    rows = 2 * TM_COMB

    @pl.when(i == 0)
    def _():
        _gather_rows(slot_ref, 0, ys_hbm, ybuf.at[0], sem.at[0], rows)

    @pl.when(i + 1 < n_steps)
    def _():
        nxt = (i + 1) % 2
        _gather_rows(slot_ref, (i + 1) * rows, ys_hbm, ybuf.at[nxt], sem.at[nxt], rows)

    cur = i % 2
    _wait_rows(ys_hbm, ybuf.at[cur], sem.at[cur], rows)
    rw = rw_ref[...]
    ya = _rows_from_tiles(ybuf.at[cur, 0:TM_COMB])
    yb = _rows_from_tiles(ybuf.at[cur, TM_COMB:rows])
    o_ref[...] = x1_ref[...] + m_ref[0, 5:6, :] * (rw[:, 0:1] * ya + rw[:, 1:2] * yb)


def _combine(slot_km, x1, mod_l, rw, ys, *, n_prompt, sample_len):
    n = x1.shape[0]
    tm = TM_COMB

    def cond_row(t, *_):
        s = t * tm
        return jnp.where(s < n_prompt, 0, 1 + (s - n_prompt) // sample_len)

    grid_spec = pltpu.PrefetchScalarGridSpec(
        num_scalar_prefetch=1,
        grid=(n // tm,),
        in_specs=[pl.BlockSpec((tm, D_MODEL), lambda t, s: (t, 0)),
                  pl.BlockSpec((1, 6, D_MODEL), lambda t, s: (cond_row(t), 0, 0)),
                  pl.BlockSpec((tm, LANES), lambda t, s: (t, 0)),
                  pl.BlockSpec(memory_space=pl.ANY)],
        out_specs=pl.BlockSpec((tm, D_MODEL), lambda t, s: (t, 0)),
        scratch_shapes=[pltpu.VMEM((2, 2 * tm, ROW_TILES, LANES), jnp.float32),
                        pltpu.SemaphoreType.DMA((2,))])
    return pl.pallas_call(
        _combine_kernel,
        out_shape=jax.ShapeDtypeStruct((n, D_MODEL), jnp.float32),
        grid_spec=grid_spec,
        compiler_params=_cparams(1),
        name="combine",
    )(slot_km, x1, mod_l, rw, ys)


def _pad_lanes(v, width=LANES):
    return jnp.pad(v, [(0, 0)] * (v.ndim - 1) + [(0, width - v.shape[-1])])


def _constants(sample_len):
    r = jnp.arange(LANES)
    c512 = jnp.arange(SSD_WIDTH)
    ef = (r[:, None] == (c512[None, :] // SSD_HEAD_DIM)).astype(MXU_DTYPE)
    eb = (r[:, None] == (c512[None, :] // SSD_HEAD_DIM + SSD_HEADS)).astype(MXU_DTYPE)
    t = jnp.arange(SSD_CHUNK)
    tril = (t[None, :] <= t[:, None]).astype(MXU_DTYPE)
    triu = (t[None, :] >= t[:, None]).astype(MXU_DTYPE)
    qk = jnp.arange(COL_V)
    bd_qk = (qk[:, None] // HEAD_DIM == qk[None, :] // HEAD_DIM).astype(MXU_DTYPE)
    cw = jnp.arange(CONV_WIDTH)
    gsz = CONV_WIDTH // CONV_GROUPS
    bd_cm = (cw[:, None] // gsz == cw[None, :] // gsz).astype(MXU_DTYPE)
    pos = jnp.arange(sample_len)
    row = (pos // GRID_W).astype(jnp.float32)
    col = (pos % GRID_W).astype(jnp.float32)
    n_freq = HEAD_DIM // 4
    inv = ROPE_THETA ** (-jnp.arange(n_freq, dtype=jnp.float32) / n_freq)
    ar = row[:, None] * inv
    ac = col[:, None] * inv
    cos64 = jnp.concatenate([jnp.cos(ar), jnp.cos(ar), jnp.cos(ac), jnp.cos(ac)], axis=1)
    sin64 = jnp.concatenate([-jnp.sin(ar), jnp.sin(ar), -jnp.sin(ac), jnp.sin(ac)], axis=1)
    cos_t = jnp.concatenate([jnp.tile(cos64, (1, 2)), jnp.ones((TM_PROJ, LANES), jnp.float32)], axis=0)
    sin_t = jnp.concatenate([jnp.tile(sin64, (1, 2)), jnp.zeros((TM_PROJ, LANES), jnp.float32)], axis=0)
    return dict(ef=ef, eb=eb, tril=tril, triu=triu, bd_qk=bd_qk, bd_cm=bd_cm, cos=cos_t, sin=sin_t)


def _pack_w_in(w_in):
    o = 0
    parts = {}
    for name, size in (("q", ATTN_WIDTH), ("k", KV_WIDTH), ("v", KV_WIDTH), ("z", SSD_WIDTH),
                       ("xbc", SSD_XBC), ("dt", SSD_HEADS), ("glu", 2 * CONV_WIDTH)):
        parts[name] = w_in[:, o:o + size]
        o += size
    dt2 = _pad_lanes(jnp.concatenate([parts["dt"], parts["dt"]], axis=1))
    return _mx(jnp.concatenate([parts["q"], parts["k"], parts["v"], parts["z"], parts["xbc"],
                                parts["glu"], dt2], axis=1))


def _layer_params(l, norm1_w, norm2_w, w_in, q_norm_w, k_norm_w, attn_sink, ssd_conv_w, ssd_conv_b,
                  ssd_dt_bias, ssd_a_log, ssd_d, ssd_norm_w, cm_conv_w, cm_conv_b, cm_norm_w, cm_norm_b,
                  w_out, router_group_w, router_group_b, router_expert_w, router_expert_b):
    rw = _pad_lanes(jnp.concatenate([router_group_w[l], router_expert_w[l]], axis=1))
    rw_hi = _mx(rw)
    rw_lo = _mx(rw - _f32(rw_hi))
    return dict(
        norm1=norm1_w[l][None, :],
        norm2=norm2_w[l][None, :],
        w_in=_pack_w_in(w_in[l]),
        qkw=jnp.concatenate([jnp.tile(q_norm_w[l], ATTN_HEADS), jnp.tile(k_norm_w[l], KV_HEADS)])[None, :],
        sink=attn_sink[l],
        ssd_conv_w=ssd_conv_w[l],
        ssd_conv_b=ssd_conv_b[l][None, :],
        dtb=_pad_lanes(ssd_dt_bias[l].reshape(1, 2 * SSD_HEADS)),
        alog=_pad_lanes(ssd_a_log[l].reshape(1, 2 * SSD_HEADS)),
        dsum=jnp.repeat(ssd_d[l, 0] + ssd_d[l, 1], SSD_HEAD_DIM)[None, :],
        ssd_norm_w=ssd_norm_w[l][None, :],
        cm_conv_w=cm_conv_w[l],
        cm_conv_b=cm_conv_b[l][None, :],
        cm_norm_w=cm_norm_w[l][None, :],
        cm_norm_b=cm_norm_b[l][None, :],
        w_out=_mx(w_out[l]),
        rw_hi=rw_hi,
        rw_lo=rw_lo,
        rb=_pad_lanes(jnp.concatenate([router_group_b[l], router_expert_b[l]])[None, :]),
    )


def _chunk_tables(n_prompt_seq, prompt_len, n_sample_seq, sample_len):
    t = SSD_CHUNK
    pc, sc = prompt_len // t, sample_len // t
    seq = [s for s in range(n_prompt_seq) for _ in range(pc)]
    seq += [n_prompt_seq + s for s in range(n_sample_seq) for _ in range(sc)]
    first = [int(i == 0) for _ in range(n_prompt_seq) for i in range(pc)]
    first += [int(i == 0) for _ in range(n_sample_seq) for i in range(sc)]
    last = [int(i == pc - 1) for _ in range(n_prompt_seq) for i in range(pc)]
    last += [int(i == sc - 1) for _ in range(n_sample_seq) for i in range(sc)]
    return (jnp.asarray(seq, jnp.int32), jnp.asarray(first, jnp.int32), jnp.asarray(last, jnp.int32))


def _layer(x, mod_l, pr, consts, tables, ck, cv, h0, wg, wu, wd, layer, *, n_prompt, prompt_len, sample_len):
    n = x.shape[0]
    n_sample = n - n_prompt
    q, kr, kn, v, z, xbc, glu, dt = _in_projection(
        x, mod_l, pr["norm1"], pr["w_in"], pr["qkw"], consts["bd_qk"], consts["cos"], consts["sin"],
        n_prompt, sample_len)
    attn_p = _attention(pr["sink"], q, kr, v, (kr, 0), (v, 0), tok_off=0, n_tok=n_prompt,
                        seq_len=prompt_len, ctx_len=prompt_len, local=False)
    attn_s = _attention(pr["sink"], q, kr, v, (ck, 0), (cv, 0), tok_off=n_prompt, n_tok=n_sample,
                        seq_len=sample_len, ctx_len=ck.shape[0] // (n_sample // sample_len), local=True)
    attn = jnp.concatenate([attn_p, attn_s], axis=0)
    ydiag, st, cdec, cm, din, u = _ssd_local(xbc, glu, dt, pr, consts, n_prompt=n_prompt,
                                             prompt_len=prompt_len, sample_len=sample_len)
    hsf, hsb, hff, hfb = _ssd_scan(*tables, st, cdec, h0)
    x1, xm, ridx, rwt = _out_projection(x, attn, u, ydiag, z, cm, din, hsf, hsb, mod_l, pr, consts,
                                        n_prompt=n_prompt, sample_len=sample_len)
    slot, src, block_e, n_valid = _route_slots(ridx)
    ys = _experts(block_e, src, n_valid, xm, wg, wu, wd, layer)
    slot_km = slot.reshape(n // TM_COMB, TM_COMB, 2).transpose(0, 2, 1).reshape(-1)
    x2 = _combine(slot_km, x1, mod_l, rwt, ys, n_prompt=n_prompt, sample_len=sample_len)
    return x2, kn, v, hff, hfb


def kernel(x_prompt, x_sample, c, cache_k, cache_v, state_ssm, c_ctx, mod_w, mod_b, norm1_w, norm2_w, w_in, q_norm_w, k_norm_w, attn_sink, ssd_conv_w, ssd_conv_b, ssd_dt_bias, ssd_a_log, ssd_d, ssd_norm_w, cm_conv_w, cm_conv_b, cm_norm_w, cm_norm_b, w_out, router_group_w, router_group_b, router_expert_w, router_expert_b, expert_w_gate, expert_w_up, expert_w_down):
    bp, lp, d = x_prompt.shape
    bs, ls, _ = x_sample.shape
    depth = mod_w.shape[0]
    past = cache_k.shape[2]
    n_prompt = bp * lp
    assert d == D_MODEL and bs + 1 <= SUBLANES
    assert lp % TM_OUT == 0 and ls % TM_OUT == 0 and lp % TM_PROJ == 0 and ls % TM_PROJ == 0

    cond8 = jnp.concatenate([c_ctx[None, :], c, jnp.zeros((SUBLANES - 1 - bs, d), jnp.float32)], axis=0)
    mod = _modulation(cond8, mod_w, mod_b).reshape(depth, SUBLANES, 6, d)
    consts = _constants(ls)
    tables = _chunk_tables(bp, lp, bs, ls)

    x = jnp.concatenate([x_prompt.reshape(n_prompt, d), x_sample.reshape(bs * ls, d)], axis=0)
    ks, vs, sts = [], [], []
    for l in range(depth):
        pr = _layer_params(l, norm1_w, norm2_w, w_in, q_norm_w, k_norm_w, attn_sink, ssd_conv_w, ssd_conv_b,
                           ssd_dt_bias, ssd_a_log, ssd_d, ssd_norm_w, cm_conv_w, cm_conv_b, cm_norm_w,
                           cm_norm_b, w_out, router_group_w, router_group_b, router_expert_w,
                           router_expert_b)
        ck = cache_k[:, l].reshape(bs * past, KV_WIDTH)
        cv = cache_v[:, l].reshape(bs * past, KV_WIDTH)
        h0s = state_ssm[:, l].transpose(0, 1, 4, 2, 3).reshape(bs, 2, SSD_STATE, SSD_WIDTH)
        h0 = jnp.concatenate([jnp.zeros((bp, 2, SSD_STATE, SSD_WIDTH), jnp.float32), h0s], axis=0)
        x, kn, v, hff, hfb = _layer(x, mod[l], pr, consts, tables, ck, cv, h0,
                                    expert_w_gate, expert_w_up, expert_w_down, l,
                                    n_prompt=n_prompt, prompt_len=lp, sample_len=ls)
        ks.append(kn[:n_prompt].reshape(bp, lp, KV_HEADS, HEAD_DIM))
        vs.append(v[:n_prompt].reshape(bp, lp, KV_HEADS, HEAD_DIM))
        hfin = jnp.concatenate([hff[:bp], hfb[:bp]], axis=1)
        sts.append(hfin.reshape(bp, 2, SSD_STATE, SSD_HEADS, SSD_HEAD_DIM).transpose(0, 1, 3, 4, 2))
    y_prompt = x[:n_prompt].reshape(bp, lp, d)
    y_sample = x[n_prompt:].reshape(bs, ls, d)
    return (y_prompt, y_sample, jnp.stack(ks, axis=1), jnp.stack(vs, axis=1), jnp.stack(sts, axis=1))
```

```python
import functools
import math

import jax
import jax.numpy as jnp
from jax import lax
from jax.experimental import pallas as pl
from jax.experimental.pallas import tpu as pltpu

D_MODEL = 1024
GRID_W = 64
HEAD_DIM = 64
ATTN_HEADS = 4
KV_HEADS = 2
ATTN_WIDTH = ATTN_HEADS * HEAD_DIM
KV_WIDTH = KV_HEADS * HEAD_DIM
WINDOW = 128
ATTN_BLOCK = 128
ROPE_THETA = 10000.0
SSD_HEAD_DIM = 64
SSD_WIDTH = 512
SSD_HEADS = 8
SSD_GROUPS = 2
SSD_STATE = 64
SSD_CONV = 5
SSD_CHUNK = 128
SSD_XBC = SSD_WIDTH + 2 * SSD_GROUPS * SSD_STATE
CONV_WIDTH = 256
CONV_KERNEL = 31
CONV_GROUPS = 4
N_EXPERT_GROUPS = 4
EXPERTS_PER_GROUP = 8
N_EXPERTS = 32
EXPERT_FF = 256
EPS = 1e-6

LANES = 128
SUBLANES = 8
VMEM_LIMIT = 52 * 1024 * 1024

MXU_DTYPE = jnp.bfloat16
TM_PROJ = 256
TM_OUT = 2 * SSD_CHUNK
MOE_BLK = 256
TM_COMB = 256
MOD_TN = 1536
HALO_SSD = 8
HALO_CM = 16
ROW_TILES = D_MODEL // LANES
COL_Q = 0
COL_K = ATTN_WIDTH
COL_V = COL_K + KV_WIDTH
COL_Z = COL_V + KV_WIDTH
COL_XBC = COL_Z + SSD_WIDTH
COL_GLU = COL_XBC + SSD_XBC
COL_DT = COL_GLU + 2 * CONV_WIDTH
PROJ_COLS = COL_DT + LANES


def _f32(x):
    return x.astype(jnp.float32)


def _mx(x):
    return x.astype(MXU_DTYPE)


def _dot(a, b):
    return jnp.dot(a, b, preferred_element_type=jnp.float32)


def _dot_nt(a, b):
    return lax.dot_general(a, b, (((1,), (1,)), ((), ())), preferred_element_type=jnp.float32)


def _split2(x):
    hi = _mx(x)
    lo = _mx(x - _f32(hi))
    return hi, lo


def _split3(x):
    hi = _mx(x)
    r = x - _f32(hi)
    mid = _mx(r)
    lo = _mx(r - _f32(mid))
    return hi, mid, lo


def _dot_sel2(x, sel):
    hi, lo = _split2(x)
    return _dot(hi, sel) + _dot(lo, sel)


def _sel_dot3(sel, x):
    hi, mid, lo = _split3(x)
    return _dot(sel, hi) + _dot(sel, mid) + _dot(sel, lo)


def _silu(x):
    return x * jax.nn.sigmoid(x)


def _softplus(x):
    return jnp.maximum(x, 0.0) + jnp.log(1.0 + jnp.exp(-jnp.abs(x)))


def _rows_to_tiles(ref, val):
    for s in range(ROW_TILES):
        ref[:, s, :] = val[:, s * LANES:(s + 1) * LANES]


def _cparams(n_axes=1):
    return pltpu.CompilerParams(dimension_semantics=("arbitrary",) * n_axes,
                                vmem_limit_bytes=VMEM_LIMIT)


def _full(shape):
    nd = len(shape)
    return pl.BlockSpec(shape, lambda *_: (0,) * nd)


def _mod_kernel(c_ref, w_ref, b_ref, o_ref):
    s = _silu(c_ref[...])
    o_ref[0] = _dot(_mx(s), _mx(w_ref[0])) + b_ref[0]


def _modulation(cond8, mod_w, mod_b):
    depth, d, n6 = mod_w.shape
    return pl.pallas_call(
        _mod_kernel,
        out_shape=jax.ShapeDtypeStruct((depth, SUBLANES, n6), jnp.float32),
        grid=(depth, n6 // MOD_TN),
        in_specs=[pl.BlockSpec((SUBLANES, d), lambda l, j: (0, 0)),
                  pl.BlockSpec((1, d, MOD_TN), lambda l, j: (l, 0, j)),
                  pl.BlockSpec((1, 1, MOD_TN), lambda l, j: (l, 0, j))],
        out_specs=pl.BlockSpec((1, SUBLANES, MOD_TN), lambda l, j: (l, 0, j)),
        compiler_params=_cparams(2),
        name="modulation",
    )(cond8, mod_w, mod_b.reshape(depth, 1, n6))


def _rope128(x, cos, sin, first_half):
    partner = jnp.where(first_half, pltpu.roll(x, LANES - 16, 1), pltpu.roll(x, 16, 1))
    return x * cos + partner * sin


def _inproj_kernel(x_ref, m_ref, n1_ref, w_ref, qkw_ref, bd_ref, cos_ref, sin_ref,
                   q_ref, kr_ref, kn_ref, v_ref, z_ref, xbc_ref, glu_ref, dt_ref):
    x = x_ref[...]
    xn = x * lax.rsqrt(jnp.mean(x * x, axis=-1, keepdims=True) + EPS) * n1_ref[...]
    h = xn * (1.0 + m_ref[0, 1:2, :]) + m_ref[0, 0:1, :]
    p = _dot(_mx(h), w_ref[...])
    qk = p[:, COL_Q:COL_V]
    seg = _dot_sel2(qk * qk, bd_ref[...])
    qk = qk * lax.rsqrt(seg * (1.0 / HEAD_DIM) + EPS) * qkw_ref[...]
    cos = cos_ref[...]
    sin = sin_ref[...]
    lane = lax.broadcasted_iota(jnp.int32, cos.shape, 1)
    first_half = (lane % 32) < 16
    scale = HEAD_DIM ** -0.5
    for c in range(ATTN_WIDTH // LANES):
        qc = _rope128(qk[:, c * LANES:(c + 1) * LANES], cos, sin, first_half)
        q_ref[:, c * LANES:(c + 1) * LANES] = (qc * scale).astype(q_ref.dtype)
    kn = qk[:, COL_K:COL_V]
    kn_ref[...] = kn
    kr_ref[...] = _rope128(kn, cos, sin, first_half).astype(kr_ref.dtype)
    v_ref[...] = p[:, COL_V:COL_Z]
    z_ref[...] = p[:, COL_Z:COL_XBC]
    xbc_ref[...] = p[:, COL_XBC:COL_GLU]
    glu_ref[...] = p[:, COL_GLU:COL_DT]
    dt_ref[...] = p[:, COL_DT:PROJ_COLS]


def _in_projection(x, mod_l, n1, w_packed, qkw, bd, cos_t, sin_t, n_prompt, sample_len):
    n = x.shape[0]
    tm = TM_PROJ
    ident_blk = sample_len // tm

    def cond_row(t):
        s = t * tm
        return jnp.where(s < n_prompt, 0, 1 + (s - n_prompt) // sample_len)

    def rope_blk(t):
        s = t * tm
        return jnp.where(s < n_prompt, ident_blk, ((s - n_prompt) % sample_len) // tm)

    def rows(width):
        return pl.BlockSpec((tm, width), lambda t: (t, 0))

    widths = (ATTN_WIDTH, KV_WIDTH, KV_WIDTH, KV_WIDTH, SSD_WIDTH, SSD_XBC, 2 * CONV_WIDTH, LANES)
    dtypes = (MXU_DTYPE, MXU_DTYPE, jnp.float32, jnp.float32, jnp.float32, jnp.float32, jnp.float32,
              jnp.float32)
    return pl.pallas_call(
        _inproj_kernel,
        out_shape=[jax.ShapeDtypeStruct((n, w), dt) for w, dt in zip(widths, dtypes)],
        grid=(n // tm,),
        in_specs=[rows(D_MODEL),
                  pl.BlockSpec((1, 6, D_MODEL), lambda t: (cond_row(t), 0, 0)),
                  _full((1, D_MODEL)),
                  _full((D_MODEL, PROJ_COLS)),
                  _full((1, COL_V)),
                  _full((COL_V, COL_V)),
                  pl.BlockSpec((tm, LANES), lambda t: (rope_blk(t), 0)),
                  pl.BlockSpec((tm, LANES), lambda t: (rope_blk(t), 0))],
        out_specs=[rows(w) for w in widths],
        compiler_params=_cparams(1),
        name="in_projection",
    )(x, mod_l, n1, w_packed, qkw, bd, cos_t, sin_t)


def _attn_kernel(sink_ref, q_ref, kc_ref, vc_ref, *rest, local, blocks_per_seq):
    if local:
        kp_ref, kcur_ref, knx_ref, vp_ref, vcur_ref, vnx_ref, o_ref = rest
    else:
        (o_ref,) = rest
    blk = ATTN_BLOCK
    j = pl.program_id(0) % blocks_per_seq
    q = q_ref[...]
    row = lax.broadcasted_iota(jnp.int32, (2 * blk, 1), 0)
    if local:
        qo = lax.broadcasted_iota(jnp.int32, (2 * blk, 3 * blk), 0) % blk
        kpos = lax.broadcasted_iota(jnp.int32, (2 * blk, 3 * blk), 1) - blk
        pos = j * blk + kpos
        valid = (jnp.abs(kpos - qo) <= WINDOW) & (pos >= 0) & (pos < blocks_per_seq * blk)
    for g in range(KV_HEADS):
        sl = slice(g * HEAD_DIM, (g + 1) * HEAD_DIM)
        q2 = jnp.concatenate([q[:, (2 * g) * HEAD_DIM:(2 * g + 1) * HEAD_DIM],
                              q[:, (2 * g + 1) * HEAD_DIM:(2 * g + 2) * HEAD_DIM]], axis=0)
        s_ctx = _dot_nt(q2, _mx(kc_ref[:, sl]))
        sink = jnp.where(row < blk, sink_ref[2 * g], sink_ref[2 * g + 1])
        m = jnp.maximum(jnp.max(s_ctx, axis=-1, keepdims=True), sink)
        if local:
            kl = jnp.concatenate([_mx(kp_ref[:, sl]), _mx(kcur_ref[:, sl]), _mx(knx_ref[:, sl])], axis=0)
            vl = jnp.concatenate([_mx(vp_ref[:, sl]), _mx(vcur_ref[:, sl]), _mx(vnx_ref[:, sl])], axis=0)
            s_loc = jnp.where(valid, _dot_nt(q2, kl), -jnp.inf)
            m = jnp.maximum(m, jnp.max(s_loc, axis=-1, keepdims=True))
        p_ctx = jnp.exp(s_ctx - m)
        den = jnp.sum(p_ctx, axis=-1, keepdims=True) + jnp.exp(sink - m)
        o = _dot(_mx(p_ctx), _mx(vc_ref[:, sl]))
        if local:
            p_loc = jnp.exp(s_loc - m)
            den = den + jnp.sum(p_loc, axis=-1, keepdims=True)
            o = o + _dot(_mx(p_loc), vl)
        o = o / den
        o_ref[:, (2 * g) * HEAD_DIM:(2 * g + 1) * HEAD_DIM] = o[:blk]
        o_ref[:, (2 * g + 1) * HEAD_DIM:(2 * g + 2) * HEAD_DIM] = o[blk:]


def _attention(sink, q, k, v, kctx, vctx, *, tok_off, n_tok, seq_len, ctx_len, local):
    blk = ATTN_BLOCK
    bps = seq_len // blk
    off = tok_off // blk
    nb = n_tok // blk
    kctx_arr, kctx_off = kctx
    vctx_arr, vctx_off = vctx
    cb_k = kctx_off // ctx_len
    cb_v = vctx_off // ctx_len

    def prev_blk(i):
        return off + (i // bps) * bps + jnp.maximum(i % bps - 1, 0)

    def next_blk(i):
        return off + (i // bps) * bps + jnp.minimum(i % bps + 1, bps - 1)

    in_specs = [pl.BlockSpec(memory_space=pltpu.SMEM),
                pl.BlockSpec((blk, ATTN_WIDTH), lambda i: (off + i, 0)),
                pl.BlockSpec((ctx_len, KV_WIDTH), lambda i: (cb_k + i // bps, 0)),
                pl.BlockSpec((ctx_len, KV_WIDTH), lambda i: (cb_v + i // bps, 0))]
    args = [sink, q, kctx_arr, vctx_arr]
    if local:
        for arr in (k, v):
            in_specs += [pl.BlockSpec((blk, KV_WIDTH), lambda i: (prev_blk(i), 0)),
                         pl.BlockSpec((blk, KV_WIDTH), lambda i: (off + i, 0)),
                         pl.BlockSpec((blk, KV_WIDTH), lambda i: (next_blk(i), 0))]
            args += [arr, arr, arr]
    return pl.pallas_call(
        functools.partial(_attn_kernel, local=local, blocks_per_seq=bps),
        out_shape=jax.ShapeDtypeStruct((n_tok, ATTN_WIDTH), jnp.float32),
        grid=(nb,),
        in_specs=in_specs,
        out_specs=pl.BlockSpec((blk, ATTN_WIDTH), lambda i: (i, 0)),
        compiler_params=_cparams(1),
        name="attention_local" if local else "attention_ctx",
    )(*args)


def _ssd_local_kernel(xbc_ref, xp_ref, xn_ref, glu_ref, gp_ref, gn_ref, dt_ref,
                      cw_ref, cb_ref, dtb_ref, alog_ref, dsum_ref, ef_ref, eb_ref, tril_ref, triu_ref,
                      mw_ref, mb_ref, nw_ref, nb_ref, bd_ref,
                      y_ref, st_ref, cdec_ref, cm_ref, din_ref, u_ref,
                      xext, gext, *, n_prompt_chunks, prompt_cps, sample_cps):
    t = SSD_CHUNK
    g = pl.program_id(0)
    in_prompt = g < n_prompt_chunks
    cps = jnp.where(in_prompt, prompt_cps, sample_cps)
    cidx = jnp.where(in_prompt, g % prompt_cps, (g - n_prompt_chunks) % sample_cps)
    not_first = (cidx > 0).astype(jnp.float32)
    not_last = (cidx < cps - 1).astype(jnp.float32)

    xext[0:HALO_SSD, :] = xp_ref[...] * not_first
    xext[HALO_SSD:HALO_SSD + t, :] = xbc_ref[...]
    xext[HALO_SSD + t:HALO_SSD + t + HALO_SSD, :] = xn_ref[...] * not_last
    half = (SSD_CONV - 1) // 2
    xc_cols = []
    for cb in range(SSD_XBC // LANES):
        cols = slice(cb * LANES, (cb + 1) * LANES)
        acc = jnp.zeros((t, LANES), jnp.float32) + cb_ref[:, cols]
        for k in range(SSD_CONV):
            acc = acc + xext[pl.ds(HALO_SSD - half + k, t), cols] * cw_ref[k:k + 1, cols]
        xc_cols.append(_silu(acc))
    xc = jnp.concatenate(xc_cols, axis=1)
    xs = xc[:, :SSD_WIDTH]
    bm = xc[:, SSD_WIDTH:SSD_WIDTH + SSD_GROUPS * SSD_STATE]
    cm = xc[:, SSD_WIDTH + SSD_GROUPS * SSD_STATE:]
    cm_ref[...] = cm

    lane = lax.broadcasted_iota(jnp.int32, (t, LANES), 1)
    fwd_lane = lane < SSD_HEADS
    dt = _softplus(dt_ref[...] + dtb_ref[...])
    la = dt * (-jnp.exp(alog_ref[...]))
    cs = jnp.where(fwd_lane, _sel_dot3(tril_ref[...], la), _sel_dot3(triu_ref[...], la))
    tot = jnp.where(fwd_lane[0:1], cs[t - 1:t, :], cs[0:1, :])
    din = jnp.exp(cs)
    din_ref[...] = din
    dec = jnp.exp(tot - cs)
    cs_row = cs.T
    ef = ef_ref[...]
    eb = eb_ref[...]
    cd = jnp.broadcast_to(jnp.exp(tot), (SUBLANES, LANES))
    cdec_ref[0] = jnp.concatenate([_dot_sel2(cd, ef)[0:1], _dot_sel2(cd, eb)[0:1],
                                   jnp.zeros((SUBLANES - 2, SSD_WIDTH), jnp.float32)], axis=0)

    bm_t = bm.T
    li = lax.broadcasted_iota(jnp.int32, (t, t), 0)
    si = lax.broadcasted_iota(jnp.int32, (t, t), 1)
    scores = [_dot_nt(_mx(cm[:, gi * SSD_STATE:(gi + 1) * SSD_STATE]),
                      _mx(bm[:, gi * SSD_STATE:(gi + 1) * SSD_STATE])) for gi in range(SSD_GROUPS)]
    hpg = SSD_HEADS // SSD_GROUPS
    gw = hpg * SSD_HEAD_DIM
    y_heads = [None] * SSD_HEADS
    for d, e_d in enumerate((ef, eb)):
        xdt = xs * _dot_sel2(dt, e_d)
        xdec = _mx(xdt * _dot_sel2(dec, e_d))
        st_ref[0, d] = jnp.concatenate(
            [_dot(_mx(bm_t[gi * SSD_STATE:(gi + 1) * SSD_STATE, :]), xdec[:, gi * gw:(gi + 1) * gw])
             for gi in range(SSD_GROUPS)], axis=1)
        mask = (si <= li) if d == 0 else (si >= li)
        for h in range(SSD_HEADS):
            c = d * SSD_HEADS + h
            seg = cs[:, c:c + 1] - cs_row[c:c + 1, :]
            lmat = jnp.exp(jnp.where(mask, seg, -jnp.inf))
            yh = _dot(_mx(scores[h // hpg] * lmat), _mx(xdt[:, h * SSD_HEAD_DIM:(h + 1) * SSD_HEAD_DIM]))
            y_heads[h] = yh if y_heads[h] is None else y_heads[h] + yh
    y_ref[...] = jnp.concatenate(y_heads, axis=1) + xs * dsum_ref[...]

    def glu(v):
        return v[:, :CONV_WIDTH] * jax.nn.sigmoid(v[:, CONV_WIDTH:])

    gext[0:HALO_CM, :] = glu(gp_ref[...]) * not_first
    gext[HALO_CM:HALO_CM + t, :] = glu(glu_ref[...])
    gext[HALO_CM + t:HALO_CM + t + HALO_CM, :] = glu(gn_ref[...]) * not_last
    halfc = (CONV_KERNEL - 1) // 2
    off0 = HALO_CM - halfc
    u_cols = []
    for cb in range(CONV_WIDTH // LANES):
        cols = slice(cb * LANES, (cb + 1) * LANES)
        ucol = jnp.zeros((t, LANES), jnp.float32) + mb_ref[:, cols]
        for r in range(SUBLANES):
            taps = [k for k in range(CONV_KERNEL) if (off0 + k) % SUBLANES == r]
            part = None
            for k in taps:
                term = gext[pl.ds(off0 + k - r, t + SUBLANES), cols] * mw_ref[k:k + 1, cols]
                part = term if part is None else part + term
            if part is not None:
                ucol = ucol + part[r:r + t, :]
        u_cols.append(ucol)
    u = jnp.concatenate(u_cols, axis=1)
    gsz = CONV_WIDTH // CONV_GROUPS
    bd = bd_ref[...]
    uc = u - _dot_sel2(u, bd) * (1.0 / gsz)
    un = uc * lax.rsqrt(_dot_sel2(uc * uc, bd) * (1.0 / gsz) + EPS)
    u_ref[...] = _silu(un * nw_ref[...] + nb_ref[...])


def _ssd_local(xbc, glu, dt, pr, consts, *, n_prompt, prompt_len, sample_len):
    n = xbc.shape[0]
    t = SSD_CHUNK
    nch = n // t
    r8 = t // HALO_SSD
    r16 = t // HALO_CM
    kern = functools.partial(_ssd_local_kernel, n_prompt_chunks=n_prompt // t,
                             prompt_cps=prompt_len // t, sample_cps=sample_len // t)
    out_shape = [jax.ShapeDtypeStruct((n, SSD_WIDTH), jnp.float32),
                 jax.ShapeDtypeStruct((nch, 2, SSD_STATE, SSD_WIDTH), jnp.float32),
                 jax.ShapeDtypeStruct((nch, SUBLANES, SSD_WIDTH), jnp.float32),
                 jax.ShapeDtypeStruct((n, LANES), jnp.float32),
                 jax.ShapeDtypeStruct((n, LANES), jnp.float32),
                 jax.ShapeDtypeStruct((n, CONV_WIDTH), jnp.float32)]
    out_specs = [pl.BlockSpec((t, SSD_WIDTH), lambda g: (g, 0)),
                 pl.BlockSpec((1, 2, SSD_STATE, SSD_WIDTH), lambda g: (g, 0, 0, 0)),
                 pl.BlockSpec((1, SUBLANES, SSD_WIDTH), lambda g: (g, 0, 0)),
                 pl.BlockSpec((t, LANES), lambda g: (g, 0)),
                 pl.BlockSpec((t, LANES), lambda g: (g, 0)),
                 pl.BlockSpec((t, CONV_WIDTH), lambda g: (g, 0))]
    in_specs = [pl.BlockSpec((t, SSD_XBC), lambda g: (g, 0)),
                pl.BlockSpec((HALO_SSD, SSD_XBC), lambda g: (jnp.maximum(g * r8 - 1, 0), 0)),
                pl.BlockSpec((HALO_SSD, SSD_XBC), lambda g: (jnp.minimum((g + 1) * r8, n // HALO_SSD - 1), 0)),
                pl.BlockSpec((t, 2 * CONV_WIDTH), lambda g: (g, 0)),
                pl.BlockSpec((HALO_CM, 2 * CONV_WIDTH), lambda g: (jnp.maximum(g * r16 - 1, 0), 0)),
                pl.BlockSpec((HALO_CM, 2 * CONV_WIDTH),
                             lambda g: (jnp.minimum((g + 1) * r16, n // HALO_CM - 1), 0)),
                pl.BlockSpec((t, LANES), lambda g: (g, 0))]
    params = [pr["ssd_conv_w"], pr["ssd_conv_b"], pr["dtb"], pr["alog"], pr["dsum"],
              consts["ef"], consts["eb"], consts["tril"], consts["triu"],
              pr["cm_conv_w"], pr["cm_conv_b"], pr["cm_norm_w"], pr["cm_norm_b"], consts["bd_cm"]]
    in_specs += [_full(p.shape) for p in params]
    return pl.pallas_call(
        kern,
        out_shape=out_shape,
        grid=(nch,),
        in_specs=in_specs,
        out_specs=out_specs,
        scratch_shapes=[pltpu.VMEM((t + 2 * HALO_SSD, SSD_XBC), jnp.float32),
                        pltpu.VMEM((t + 2 * HALO_CM, CONV_WIDTH), jnp.float32)],
        compiler_params=_cparams(1),
        name="ssd_local",
    )(xbc, xbc, xbc, glu, glu, glu, dt, *params)


def _ssd_scan_kernel(seq_ref, first_ref, last_ref, stf_ref, stb_ref, cdf_ref, cdb_ref, h0f_ref, h0b_ref,
                     hsf_ref, hsb_ref, hff_ref, hfb_ref, hf, hb):
    g = pl.program_id(0)
    gb = pl.num_programs(0) - 1 - g

    @pl.when(first_ref[g] == 1)
    def _():
        hf[...] = h0f_ref[0, 0]

    @pl.when(last_ref[gb] == 1)
    def _():
        hb[...] = h0b_ref[0, 0]

    hsf_ref[0] = hf[...]
    hf_new = hf[...] * cdf_ref[0, 0:1, :] + stf_ref[0, 0]
    hf[...] = hf_new
    hff_ref[0, 0] = hf_new
    hsb_ref[0] = hb[...]
    hb_new = hb[...] * cdb_ref[0, 1:2, :] + stb_ref[0, 0]
    hb[...] = hb_new
    hfb_ref[0, 0] = hb_new


def _ssd_scan(seq_id, first, last, st, cdec, h0):
    nch = st.shape[0]
    nseq = h0.shape[0]
    blk4 = (1, 1, SSD_STATE, SSD_WIDTH)
    grid_spec = pltpu.PrefetchScalarGridSpec(
        num_scalar_prefetch=3,
        grid=(nch,),
        in_specs=[pl.BlockSpec(blk4, lambda g, s, f, l: (g, 0, 0, 0)),
                  pl.BlockSpec(blk4, lambda g, s, f, l: (nch - 1 - g, 1, 0, 0)),
                  pl.BlockSpec((1, SUBLANES, SSD_WIDTH), lambda g, s, f, l: (g, 0, 0)),
                  pl.BlockSpec((1, SUBLANES, SSD_WIDTH), lambda g, s, f, l: (nch - 1 - g, 0, 0)),
                  pl.BlockSpec(blk4, lambda g, s, f, l: (s[g], 0, 0, 0)),
                  pl.BlockSpec(blk4, lambda g, s, f, l: (s[nch - 1 - g], 1, 0, 0))],
        out_specs=[pl.BlockSpec((1, SSD_STATE, SSD_WIDTH), lambda g, s, f, l: (g, 0, 0)),
                   pl.BlockSpec((1, SSD_STATE, SSD_WIDTH), lambda g, s, f, l: (nch - 1 - g, 0, 0)),
                   pl.BlockSpec(blk4, lambda g, s, f, l: (s[g], 0, 0, 0)),
                   pl.BlockSpec(blk4, lambda g, s, f, l: (s[nch - 1 - g], 0, 0, 0))],
        scratch_shapes=[pltpu.VMEM((SSD_STATE, SSD_WIDTH), jnp.float32),
                        pltpu.VMEM((SSD_STATE, SSD_WIDTH), jnp.float32)])
    return pl.pallas_call(
        _ssd_scan_kernel,
        out_shape=[jax.ShapeDtypeStruct((nch, SSD_STATE, SSD_WIDTH), jnp.float32),
                   jax.ShapeDtypeStruct((nch, SSD_STATE, SSD_WIDTH), jnp.float32),
                   jax.ShapeDtypeStruct((nseq, 1, SSD_STATE, SSD_WIDTH), jnp.float32),
                   jax.ShapeDtypeStruct((nseq, 1, SSD_STATE, SSD_WIDTH), jnp.float32)],
        grid_spec=grid_spec,
        compiler_params=_cparams(1),
        name="ssd_scan",
    )(seq_id, first, last, st, st, cdec, cdec, h0, h0)


def _outproj_kernel(x_ref, attn_ref, u_ref, y_ref, z_ref, cm_ref, din_ref, hsf_ref, hsb_ref, m_ref,
                    sw_ref, wo_ref, n2_ref, rwh_ref, rwl_ref, rb_ref, ef_ref, eb_ref,
                    x1_ref, xm_ref, ri_ref, rw_ref):
    t = SSD_CHUNK
    hpg = SSD_HEADS // SSD_GROUPS
    gw = hpg * SSD_HEAD_DIM
    ys = []
    for c in range(TM_OUT // t):
        rs = slice(c * t, (c + 1) * t)
        cmc = cm_ref[rs, :]
        dinc = din_ref[rs, :]
        y = y_ref[rs, :]
        for hs_ref, e_ref in ((hsf_ref, ef_ref), (hsb_ref, eb_ref)):
            hs = hs_ref[c]
            yoff = jnp.concatenate(
                [_dot(_mx(cmc[:, gi * SSD_STATE:(gi + 1) * SSD_STATE]), _mx(hs[:, gi * gw:(gi + 1) * gw]))
                 for gi in range(SSD_GROUPS)], axis=1)
            y = y + yoff * _dot_sel2(dinc, e_ref[...])
        ys.append(y)
    y = jnp.concatenate(ys, axis=0) * _silu(z_ref[...])
    nrm = []
    for gi in range(SSD_GROUPS):
        yg = y[:, gi * gw:(gi + 1) * gw]
        nrm.append(yg * lax.rsqrt(jnp.mean(yg * yg, axis=-1, keepdims=True) + EPS))
    y = jnp.concatenate(nrm, axis=1) * sw_ref[...]
    o1 = ATTN_WIDTH
    o2 = ATTN_WIDTH + SSD_WIDTH
    mix = (_dot(_mx(attn_ref[...]), wo_ref[0:o1, :]) + _dot(_mx(y), wo_ref[o1:o2, :])
           + _dot(_mx(u_ref[...]), wo_ref[o2:, :]))
    x1 = x_ref[...] + m_ref[0, 2:3, :] * mix
    x1_ref[...] = x1
    xm = (x1 * lax.rsqrt(jnp.mean(x1 * x1, axis=-1, keepdims=True) + EPS) * n2_ref[...]
          * (1.0 + m_ref[0, 4:5, :]) + m_ref[0, 3:4, :])
    _rows_to_tiles(xm_ref, xm)
    xh, xl = _split2(xm)
    lg = _dot(xh, rwh_ref[...]) + _dot(xh, rwl_ref[...]) + _dot(xl, rwh_ref[...]) + rb_ref[...]
    lane_i = lax.broadcasted_iota(jnp.int32, lg.shape, 1)
    lane = lane_i.astype(jnp.float32)
    ninf = -jnp.inf
    big = float(LANES)
    is_g = lane_i < N_EXPERT_GROUPS
    gmax = jnp.max(jnp.where(is_g, lg, ninf), axis=-1, keepdims=True)
    gidx = jnp.min(jnp.where(is_g & (lg == gmax), lane, big), axis=-1, keepdims=True)
    p_grp = 1.0 / jnp.sum(jnp.where(is_g, jnp.exp(lg - gmax), 0.0), axis=-1, keepdims=True)
    lo = N_EXPERT_GROUPS + EXPERTS_PER_GROUP * gidx
    in_e = (lane >= lo) & (lane < lo + EXPERTS_PER_GROUP)
    v1 = jnp.max(jnp.where(in_e, lg, ninf), axis=-1, keepdims=True)
    i1 = jnp.min(jnp.where(in_e & (lg == v1), lane, big), axis=-1, keepdims=True)
    rest = in_e & (lane != i1)
    v2 = jnp.max(jnp.where(rest, lg, ninf), axis=-1, keepdims=True)
    i2 = jnp.min(jnp.where(rest & (lg == v2), lane, big), axis=-1, keepdims=True)
    e2 = jnp.exp(v2 - v1)
    w1 = p_grp * (1.0 / (1.0 + e2))
    w2 = p_grp * (e2 / (1.0 + e2))
    e_first = (i1 - N_EXPERT_GROUPS).astype(jnp.int32)
    e_second = (i2 - N_EXPERT_GROUPS).astype(jnp.int32)
    ri_ref[...] = jnp.where(lane_i == 0, e_first, jnp.where(lane_i == 1, e_second, 0))
    rw_ref[...] = jnp.where(lane_i == 0, w1, jnp.where(lane_i == 1, w2, 0.0))


def _out_projection(x, attn, u, ydiag, z, cm, din, hsf, hsb, mod_l, pr, consts, *, n_prompt, sample_len):
    n = x.shape[0]
    tm = TM_OUT
    cpt = tm // SSD_CHUNK

    def cond_row(t):
        s = t * tm
        return jnp.where(s < n_prompt, 0, 1 + (s - n_prompt) // sample_len)

    def rows(width):
        return pl.BlockSpec((tm, width), lambda t: (t, 0))

    params = [pr["ssd_norm_w"], pr["w_out"], pr["norm2"], pr["rw_hi"], pr["rw_lo"], pr["rb"],
              consts["ef"], consts["eb"]]
    return pl.pallas_call(
        _outproj_kernel,
        out_shape=[jax.ShapeDtypeStruct((n, D_MODEL), jnp.float32),
                   jax.ShapeDtypeStruct((n, ROW_TILES, LANES), jnp.float32),
                   jax.ShapeDtypeStruct((n, LANES), jnp.int32),
                   jax.ShapeDtypeStruct((n, LANES), jnp.float32)],
        grid=(n // tm,),
        in_specs=[rows(D_MODEL), rows(ATTN_WIDTH), rows(CONV_WIDTH), rows(SSD_WIDTH), rows(SSD_WIDTH),
                  rows(LANES), rows(LANES),
                  pl.BlockSpec((cpt, SSD_STATE, SSD_WIDTH), lambda t: (t, 0, 0)),
                  pl.BlockSpec((cpt, SSD_STATE, SSD_WIDTH), lambda t: (t, 0, 0)),
                  pl.BlockSpec((1, 6, D_MODEL), lambda t: (cond_row(t), 0, 0))]
        + [_full(p.shape) for p in params],
        out_specs=[rows(D_MODEL), pl.BlockSpec((tm, ROW_TILES, LANES), lambda t: (t, 0, 0)),
                   rows(LANES), rows(LANES)],
        compiler_params=_cparams(1),
        name="out_projection",
    )(x, attn, u, ydiag, z, cm, din, hsf, hsb, mod_l, *params)


def _route_slots(ridx):
    n = ridx.shape[0]
    n_assign = 2 * n
    blk = MOE_BLK
    flat = ridx[:, :2].reshape(-1)
    onehot = (flat[:, None] == jnp.arange(N_EXPERTS, dtype=jnp.int32)[None, :]).astype(jnp.int32)
    csum = jnp.cumsum(onehot, axis=0)
    counts = csum[-1]
    padded = (counts + blk - 1) // blk * blk
    pad_end = jnp.cumsum(padded)
    pad_start = pad_end - padded
    slot = jnp.sum(onehot * (csum - 1 + pad_start[None, :]), axis=1).astype(jnp.int32)
    n_blocks = n_assign // blk + N_EXPERTS
    src = jnp.zeros((n_blocks * blk,), jnp.int32).at[slot].set(jnp.arange(n_assign, dtype=jnp.int32) // 2)
    starts = jnp.arange(n_blocks, dtype=jnp.int32) * blk
    block_e = jnp.minimum(jnp.sum((pad_end[None, :] <= starts[:, None]).astype(jnp.int32), axis=1),
                          N_EXPERTS - 1).astype(jnp.int32)
    n_valid = (pad_end[-1] // blk).astype(jnp.int32).reshape(1)
    return slot, src, block_e, n_valid


def _gather_rows(idx_ref, base, src_hbm, dst, sem, n_rows):
    def body(j, carry):
        for u in range(SUBLANES):
            r = idx_ref[base + j * SUBLANES + u]
            pltpu.make_async_copy(src_hbm.at[r], dst.at[j, :, u, :], sem).start()
        return carry

    lax.fori_loop(0, n_rows // SUBLANES, body, 0)


def _wait_rows(dst, sem):
    pltpu.make_async_copy(dst, dst, sem).wait()


def _rows_from_groups(ref, g0, g1):
    return jnp.concatenate(
        [jnp.concatenate([ref[g, s] for s in range(ROW_TILES)], axis=1) for g in range(g0, g1)], axis=0)


def _experts_kernel(be_ref, src_ref, nv_ref, x_hbm, wg_ref, wu_ref, wd_ref, o_ref,
                    xbuf, sem, wg_c, wu_c, wd_c):
    b = pl.program_id(0)
    nv = nv_ref[0]
    blk = MOE_BLK

    @pl.when(b == 0)
    def _():
        _gather_rows(src_ref, 0, x_hbm, xbuf.at[0], sem.at[0], blk)

    @pl.when(b + 1 < nv)
    def _():
        nxt = (b + 1) % 2
        _gather_rows(src_ref, (b + 1) * blk, x_hbm, xbuf.at[nxt], sem.at[nxt], blk)

    @pl.when((b == 0) | (be_ref[b] != be_ref[jnp.maximum(b - 1, 0)]))
    def _():
        wg_c[...] = _mx(wg_ref[0, 0])
        wu_c[...] = _mx(wu_ref[0, 0])
        wd_c[...] = _mx(wd_ref[0, 0])

    @pl.when(b < nv)
    def _():
        cur = b % 2
        _wait_rows(xbuf.at[cur], sem.at[cur])
        xb = _mx(_rows_from_groups(xbuf.at[cur], 0, blk // SUBLANES))
        hid = _silu(_dot(xb, wg_c[...])) * _dot(xb, wu_c[...])
        _rows_to_tiles(o_ref, _dot(_mx(hid), wd_c[...]))

    @pl.when(b >= nv)
    def _():
        o_ref[...] = jnp.zeros_like(o_ref)


def _experts(block_e, src, n_valid, xm, wg, wu, wd, layer):
    blk = MOE_BLK
    n_blocks = block_e.shape[0]
    grid_spec = pltpu.PrefetchScalarGridSpec(
        num_scalar_prefetch=3,
        grid=(n_blocks,),
        in_specs=[pl.BlockSpec(memory_space=pl.ANY),
                  pl.BlockSpec((1, 1, D_MODEL, EXPERT_FF), lambda b, be, s, nv: (layer, be[b], 0, 0)),
                  pl.BlockSpec((1, 1, D_MODEL, EXPERT_FF), lambda b, be, s, nv: (layer, be[b], 0, 0)),
                  pl.BlockSpec((1, 1, EXPERT_FF, D_MODEL), lambda b, be, s, nv: (layer, be[b], 0, 0))],
        out_specs=pl.BlockSpec((blk, ROW_TILES, LANES), lambda b, be, s, nv: (b, 0, 0)),
        scratch_shapes=[pltpu.VMEM((2, blk // SUBLANES, ROW_TILES, SUBLANES, LANES), jnp.float32),
                        pltpu.SemaphoreType.DMA((2,)),
                        pltpu.VMEM((D_MODEL, EXPERT_FF), MXU_DTYPE),
                        pltpu.VMEM((D_MODEL, EXPERT_FF), MXU_DTYPE),
                        pltpu.VMEM((EXPERT_FF, D_MODEL), MXU_DTYPE)])
    return pl.pallas_call(
        _experts_kernel,
        out_shape=jax.ShapeDtypeStruct((n_blocks * blk, ROW_TILES, LANES), jnp.float32),
        grid_spec=grid_spec,
        compiler_params=_cparams(1),
        name="experts",
    )(block_e, src, n_valid, xm, wg, wu, wd)


def _combine_kernel(slot_ref, x1_ref, m_ref, rw_ref, ys_hbm, o_ref, ybuf, sem):
    i = pl.program_id(0)
    n_steps = pl.num_programs(0)
    rows = 2 * TM_COMB

    @pl.when(i == 0)
    def _():
        _gather_rows(slot_ref, 0, ys_hbm, ybuf.at[0], sem.at[0], rows)

    @pl.when(i + 1 < n_steps)
    def _():
        nxt = (i + 1) % 2
        _gather_rows(slot_ref, (i + 1) * rows, ys_hbm, ybuf.at[nxt], sem.at[nxt], rows)

    cur = i % 2
    _wait_rows(ybuf.at[cur], sem.at[cur])
    rw = rw_ref[...]
    groups = TM_COMB // SUBLANES
    ya = _rows_from_groups(ybuf.at[cur], 0, groups)
    yb = _rows_from_groups(ybuf.at[cur], groups, 2 * groups)
    o_ref[...] = x1_ref[...] + m_ref[0, 5:6, :] * (rw[:, 0:1] * ya + rw[:, 1:2] * yb)


def _combine(slot_km, x1, mod_l, rw, ys, *, n_prompt, sample_len):
    n = x1.shape[0]
    tm = TM_COMB

    def cond_row(t, *_):
        s = t * tm
        return jnp.where(s < n_prompt, 0, 1 + (s - n_prompt) // sample_len)

    grid_spec = pltpu.PrefetchScalarGridSpec(
        num_scalar_prefetch=1,
        grid=(n // tm,),
        in_specs=[pl.BlockSpec((tm, D_MODEL), lambda t, s: (t, 0)),
                  pl.BlockSpec((1, 6, D_MODEL), lambda t, s: (cond_row(t), 0, 0)),
                  pl.BlockSpec((tm, LANES), lambda t, s: (t, 0)),
                  pl.BlockSpec(memory_space=pl.ANY)],
        out_specs=pl.BlockSpec((tm, D_MODEL), lambda t, s: (t, 0)),
        scratch_shapes=[pltpu.VMEM((2, 2 * tm // SUBLANES, ROW_TILES, SUBLANES, LANES), jnp.float32),
                        pltpu.SemaphoreType.DMA((2,))])
    return pl.pallas_call(
        _combine_kernel,
        out_shape=jax.ShapeDtypeStruct((n, D_MODEL), jnp.float32),
        grid_spec=grid_spec,
        compiler_params=_cparams(1),
        name="combine",
    )(slot_km, x1, mod_l, rw, ys)


def _pad_lanes(v, width=LANES):
    return jnp.pad(v, [(0, 0)] * (v.ndim - 1) + [(0, width - v.shape[-1])])


def _constants(sample_len):
    r = jnp.arange(LANES)
    c512 = jnp.arange(SSD_WIDTH)
    ef = (r[:, None] == (c512[None, :] // SSD_HEAD_DIM)).astype(MXU_DTYPE)
    eb = (r[:, None] == (c512[None, :] // SSD_HEAD_DIM + SSD_HEADS)).astype(MXU_DTYPE)
    t = jnp.arange(SSD_CHUNK)
    tril = (t[None, :] <= t[:, None]).astype(MXU_DTYPE)
    triu = (t[None, :] >= t[:, None]).astype(MXU_DTYPE)
    qk = jnp.arange(COL_V)
    bd_qk = (qk[:, None] // HEAD_DIM == qk[None, :] // HEAD_DIM).astype(MXU_DTYPE)
    cw = jnp.arange(CONV_WIDTH)
    gsz = CONV_WIDTH // CONV_GROUPS
    bd_cm = (cw[:, None] // gsz == cw[None, :] // gsz).astype(MXU_DTYPE)
    pos = jnp.arange(sample_len)
    row = (pos // GRID_W).astype(jnp.float32)
    col = (pos % GRID_W).astype(jnp.float32)
    n_freq = HEAD_DIM // 4
    inv = ROPE_THETA ** (-jnp.arange(n_freq, dtype=jnp.float32) / n_freq)
    ar = row[:, None] * inv
    ac = col[:, None] * inv
    cos64 = jnp.concatenate([jnp.cos(ar), jnp.cos(ar), jnp.cos(ac), jnp.cos(ac)], axis=1)
    sin64 = jnp.concatenate([-jnp.sin(ar), jnp.sin(ar), -jnp.sin(ac), jnp.sin(ac)], axis=1)
    cos_t = jnp.concatenate([jnp.tile(cos64, (1, 2)), jnp.ones((TM_PROJ, LANES), jnp.float32)], axis=0)
    sin_t = jnp.concatenate([jnp.tile(sin64, (1, 2)), jnp.zeros((TM_PROJ, LANES), jnp.float32)], axis=0)
    return dict(ef=ef, eb=eb, tril=tril, triu=triu, bd_qk=bd_qk, bd_cm=bd_cm, cos=cos_t, sin=sin_t)


def _pack_w_in(w_in):
    o = 0
    parts = {}
    for name, size in (("q", ATTN_WIDTH), ("k", KV_WIDTH), ("v", KV_WIDTH), ("z", SSD_WIDTH),
                       ("xbc", SSD_XBC), ("dt", SSD_HEADS), ("glu", 2 * CONV_WIDTH)):
        parts[name] = w_in[:, o:o + size]
        o += size
    dt2 = _pad_lanes(jnp.concatenate([parts["dt"], parts["dt"]], axis=1))
    return _mx(jnp.concatenate([parts["q"], parts["k"], parts["v"], parts["z"], parts["xbc"],
                                parts["glu"], dt2], axis=1))


def _layer_params(l, norm1_w, norm2_w, w_in, q_norm_w, k_norm_w, attn_sink, ssd_conv_w, ssd_conv_b,
                  ssd_dt_bias, ssd_a_log, ssd_d, ssd_norm_w, cm_conv_w, cm_conv_b, cm_norm_w, cm_norm_b,
                  w_out, router_group_w, router_group_b, router_expert_w, router_expert_b):
    rw = _pad_lanes(jnp.concatenate([router_group_w[l], router_expert_w[l]], axis=1))
    rw_hi = _mx(rw)
    rw_lo = _mx(rw - _f32(rw_hi))
    return dict(
        norm1=norm1_w[l][None, :],
        norm2=norm2_w[l][None, :],
        w_in=_pack_w_in(w_in[l]),
        qkw=jnp.concatenate([jnp.tile(q_norm_w[l], ATTN_HEADS), jnp.tile(k_norm_w[l], KV_HEADS)])[None, :],
        sink=attn_sink[l],
        ssd_conv_w=ssd_conv_w[l],
        ssd_conv_b=ssd_conv_b[l][None, :],
        dtb=_pad_lanes(ssd_dt_bias[l].reshape(1, 2 * SSD_HEADS)),
        alog=_pad_lanes(ssd_a_log[l].reshape(1, 2 * SSD_HEADS)),
        dsum=jnp.repeat(ssd_d[l, 0] + ssd_d[l, 1], SSD_HEAD_DIM)[None, :],
        ssd_norm_w=ssd_norm_w[l][None, :],
        cm_conv_w=cm_conv_w[l],
        cm_conv_b=cm_conv_b[l][None, :],
        cm_norm_w=cm_norm_w[l][None, :],
        cm_norm_b=cm_norm_b[l][None, :],
        w_out=_mx(w_out[l]),
        rw_hi=rw_hi,
        rw_lo=rw_lo,
        rb=_pad_lanes(jnp.concatenate([router_group_b[l], router_expert_b[l]])[None, :]),
    )


def _chunk_tables(n_prompt_seq, prompt_len, n_sample_seq, sample_len):
    t = SSD_CHUNK
    pc, sc = prompt_len // t, sample_len // t
    seq = [s for s in range(n_prompt_seq) for _ in range(pc)]
    seq += [n_prompt_seq + s for s in range(n_sample_seq) for _ in range(sc)]
    first = [int(i == 0) for _ in range(n_prompt_seq) for i in range(pc)]
    first += [int(i == 0) for _ in range(n_sample_seq) for i in range(sc)]
    last = [int(i == pc - 1) for _ in range(n_prompt_seq) for i in range(pc)]
    last += [int(i == sc - 1) for _ in range(n_sample_seq) for i in range(sc)]
    return (jnp.asarray(seq, jnp.int32), jnp.asarray(first, jnp.int32), jnp.asarray(last, jnp.int32))


def _layer(x, mod_l, pr, consts, tables, ck, cv, h0, wg, wu, wd, layer, *, n_prompt, prompt_len, sample_len):
    n = x.shape[0]
    n_sample = n - n_prompt
    q, kr, kn, v, z, xbc, glu, dt = _in_projection(
        x, mod_l, pr["norm1"], pr["w_in"], pr["qkw"], consts["bd_qk"], consts["cos"], consts["sin"],
        n_prompt, sample_len)
    attn_p = _attention(pr["sink"], q, kr, v, (kr, 0), (v, 0), tok_off=0, n_tok=n_prompt,
                        seq_len=prompt_len, ctx_len=prompt_len, local=False)
    attn_s = _attention(pr["sink"], q, kr, v, (ck, 0), (cv, 0), tok_off=n_prompt, n_tok=n_sample,
                        seq_len=sample_len, ctx_len=ck.shape[0] // (n_sample // sample_len), local=True)
    attn = jnp.concatenate([attn_p, attn_s], axis=0)
    ydiag, st, cdec, cm, din, u = _ssd_local(xbc, glu, dt, pr, consts, n_prompt=n_prompt,
                                             prompt_len=prompt_len, sample_len=sample_len)
    hsf, hsb, hff, hfb = _ssd_scan(*tables, st, cdec, h0)
    x1, xm, ridx, rwt = _out_projection(x, attn, u, ydiag, z, cm, din, hsf, hsb, mod_l, pr, consts,
                                        n_prompt=n_prompt, sample_len=sample_len)
    slot, src, block_e, n_valid = _route_slots(ridx)
    ys = _experts(block_e, src, n_valid, xm, wg, wu, wd, layer)
    slot_km = slot.reshape(n // TM_COMB, TM_COMB, 2).transpose(0, 2, 1).reshape(-1)
    x2 = _combine(slot_km, x1, mod_l, rwt, ys, n_prompt=n_prompt, sample_len=sample_len)
    return x2, kn, v, hff, hfb


def kernel(x_prompt, x_sample, c, cache_k, cache_v, state_ssm, c_ctx, mod_w, mod_b, norm1_w, norm2_w, w_in, q_norm_w, k_norm_w, attn_sink, ssd_conv_w, ssd_conv_b, ssd_dt_bias, ssd_a_log, ssd_d, ssd_norm_w, cm_conv_w, cm_conv_b, cm_norm_w, cm_norm_b, w_out, router_group_w, router_group_b, router_expert_w, router_expert_b, expert_w_gate, expert_w_up, expert_w_down):
    bp, lp, d = x_prompt.shape
    bs, ls, _ = x_sample.shape
    depth = mod_w.shape[0]
    past = cache_k.shape[2]
    n_prompt = bp * lp
    assert d == D_MODEL and bs + 1 <= SUBLANES
    assert lp % TM_OUT == 0 and ls % TM_OUT == 0 and lp % TM_PROJ == 0 and ls % TM_PROJ == 0

    cond8 = jnp.concatenate([c_ctx[None, :], c, jnp.zeros((SUBLANES - 1 - bs, d), jnp.float32)], axis=0)
    mod = _modulation(cond8, mod_w, mod_b).reshape(depth, SUBLANES, 6, d)
    consts = _constants(ls)
    tables = _chunk_tables(bp, lp, bs, ls)

    x = jnp.concatenate([x_prompt.reshape(n_prompt, d), x_sample.reshape(bs * ls, d)], axis=0)
    ks, vs, sts = [], [], []
    for l in range(depth):
        pr = _layer_params(l, norm1_w, norm2_w, w_in, q_norm_w, k_norm_w, attn_sink, ssd_conv_w, ssd_conv_b,
                           ssd_dt_bias, ssd_a_log, ssd_d, ssd_norm_w, cm_conv_w, cm_conv_b, cm_norm_w,
                           cm_norm_b, w_out, router_group_w, router_group_b, router_expert_w,
                           router_expert_b)
        ck = cache_k[:, l].reshape(bs * past, KV_WIDTH)
        cv = cache_v[:, l].reshape(bs * past, KV_WIDTH)
        h0s = state_ssm[:, l].transpose(0, 1, 4, 2, 3).reshape(bs, 2, SSD_STATE, SSD_WIDTH)
        h0 = jnp.concatenate([jnp.zeros((bp, 2, SSD_STATE, SSD_WIDTH), jnp.float32), h0s], axis=0)
        x, kn, v, hff, hfb = _layer(x, mod[l], pr, consts, tables, ck, cv, h0,
                                    expert_w_gate, expert_w_up, expert_w_down, l,
                                    n_prompt=n_prompt, prompt_len=lp, sample_len=ls)
        ks.append(kn[:n_prompt].reshape(bp, lp, KV_HEADS, HEAD_DIM))
        vs.append(v[:n_prompt].reshape(bp, lp, KV_HEADS, HEAD_DIM))
        hfin = jnp.concatenate([hff[:bp], hfb[:bp]], axis=1)
        sts.append(hfin.reshape(bp, 2, SSD_STATE, SSD_HEADS, SSD_HEAD_DIM).transpose(0, 1, 3, 4, 2))
    y_prompt = x[:n_prompt].reshape(bp, lp, d)
    y_sample = x[n_prompt:].reshape(bs, ls, d)
    return (y_prompt, y_sample, jnp.stack(ks, axis=1), jnp.stack(vs, axis=1), jnp.stack(sts, axis=1))
```

```python
import functools
import math

import jax
import jax.numpy as jnp
from jax import lax
from jax.experimental import pallas as pl
from jax.experimental.pallas import tpu as pltpu

D_MODEL = 1024
GRID_W = 64
HEAD_DIM = 64
ATTN_HEADS = 4
KV_HEADS = 2
ATTN_WIDTH = ATTN_HEADS * HEAD_DIM
KV_WIDTH = KV_HEADS * HEAD_DIM
WINDOW = 128
ATTN_BLOCK = 128
ROPE_THETA = 10000.0
SSD_HEAD_DIM = 64
SSD_WIDTH = 512
SSD_HEADS = 8
SSD_GROUPS = 2
SSD_STATE = 64
SSD_CONV = 5
SSD_CHUNK = 128
SSD_XBC = SSD_WIDTH + 2 * SSD_GROUPS * SSD_STATE
CONV_WIDTH = 256
CONV_KERNEL = 31
CONV_GROUPS = 4
N_EXPERT_GROUPS = 4
EXPERTS_PER_GROUP = 8
N_EXPERTS = 32
EXPERT_FF = 256
EPS = 1e-6

LANES = 128
SUBLANES = 8
VMEM_LIMIT = 52 * 1024 * 1024

MXU_DTYPE = jnp.bfloat16
TM_PROJ = 512
TM_OUT = 4 * SSD_CHUNK
SUB_OUT = 2 * SSD_CHUNK
SCAN_CHUNKS = 2
MOE_BLK = 256
TM_COMB = 256
MOD_TN = 1536
HALO_SSD = 8
HALO_CM = 16
ROW_TILES = D_MODEL // LANES
COL_Q = 0
COL_K = ATTN_WIDTH
COL_V = COL_K + KV_WIDTH
COL_Z = COL_V + KV_WIDTH
COL_XBC = COL_Z + SSD_WIDTH
COL_GLU = COL_XBC + SSD_XBC
COL_DT = COL_GLU + 2 * CONV_WIDTH
PROJ_COLS = COL_DT + LANES


def _f32(x):
    return x.astype(jnp.float32)


def _mx(x):
    return x.astype(MXU_DTYPE)


def _dot(a, b):
    return jnp.dot(a, b, preferred_element_type=jnp.float32)


def _dot_nt(a, b):
    return lax.dot_general(a, b, (((1,), (1,)), ((), ())), preferred_element_type=jnp.float32)


def _split2(x):
    hi = _mx(x)
    lo = _mx(x - _f32(hi))
    return hi, lo


def _split3(x):
    hi = _mx(x)
    r = x - _f32(hi)
    mid = _mx(r)
    lo = _mx(r - _f32(mid))
    return hi, mid, lo


def _dot_sel2(x, sel):
    hi, lo = _split2(x)
    return _dot(hi, sel) + _dot(lo, sel)


def _sel_dot3(sel, x):
    hi, mid, lo = _split3(x)
    return _dot(sel, hi) + _dot(sel, mid) + _dot(sel, lo)


def _silu(x):
    return x * jax.nn.sigmoid(x)


def _softplus(x):
    return jnp.maximum(x, 0.0) + jnp.log(1.0 + jnp.exp(-jnp.abs(x)))


def _rows_to_tiles(ref, val):
    for s in range(ROW_TILES):
        ref[:, s, :] = val[:, s * LANES:(s + 1) * LANES]


def _cparams(n_axes=1):
    return pltpu.CompilerParams(dimension_semantics=("arbitrary",) * n_axes,
                                vmem_limit_bytes=VMEM_LIMIT)


def _full(shape):
    nd = len(shape)
    return pl.BlockSpec(shape, lambda *_: (0,) * nd)


def _mod_kernel(c_ref, w_ref, b_ref, o_ref):
    s = _silu(c_ref[...])
    o_ref[0] = _dot(_mx(s), _mx(w_ref[0])) + b_ref[0]


def _modulation(cond8, mod_w, mod_b):
    depth, d, n6 = mod_w.shape
    return pl.pallas_call(
        _mod_kernel,
        out_shape=jax.ShapeDtypeStruct((depth, SUBLANES, n6), jnp.float32),
        grid=(depth, n6 // MOD_TN),
        in_specs=[pl.BlockSpec((SUBLANES, d), lambda l, j: (0, 0)),
                  pl.BlockSpec((1, d, MOD_TN), lambda l, j: (l, 0, j)),
                  pl.BlockSpec((1, 1, MOD_TN), lambda l, j: (l, 0, j))],
        out_specs=pl.BlockSpec((1, SUBLANES, MOD_TN), lambda l, j: (l, 0, j)),
        compiler_params=_cparams(2),
        name="modulation",
    )(cond8, mod_w, mod_b.reshape(depth, 1, n6))


def _rope128(x, cos, sin, first_half):
    partner = jnp.where(first_half, pltpu.roll(x, LANES - 16, 1), pltpu.roll(x, 16, 1))
    return x * cos + partner * sin


def _inproj_kernel(x_ref, m_ref, n1_ref, w_ref, qkw_ref, bd_ref, cos_ref, sin_ref,
                   q_ref, kr_ref, kn_ref, v_ref, z_ref, xbc_ref, glu_ref, dt_ref):
    x = x_ref[...]
    xn = x * lax.rsqrt(jnp.mean(x * x, axis=-1, keepdims=True) + EPS) * n1_ref[...]
    h = xn * (1.0 + m_ref[0, 1:2, :]) + m_ref[0, 0:1, :]
    p = _dot(_mx(h), w_ref[...])
    qk = p[:, COL_Q:COL_V]
    seg = _dot_sel2(qk * qk, bd_ref[...])
    qk = qk * lax.rsqrt(seg * (1.0 / HEAD_DIM) + EPS) * qkw_ref[...]
    cos = cos_ref[...]
    sin = sin_ref[...]
    lane = lax.broadcasted_iota(jnp.int32, cos.shape, 1)
    first_half = (lane % 32) < 16
    scale = HEAD_DIM ** -0.5
    for c in range(ATTN_WIDTH // LANES):
        qc = _rope128(qk[:, c * LANES:(c + 1) * LANES], cos, sin, first_half)
        q_ref[:, c * LANES:(c + 1) * LANES] = (qc * scale).astype(q_ref.dtype)
    kn = qk[:, COL_K:COL_V]
    kn_ref[...] = kn
    kr_ref[...] = _rope128(kn, cos, sin, first_half).astype(kr_ref.dtype)
    v_ref[...] = p[:, COL_V:COL_Z]
    z_ref[...] = p[:, COL_Z:COL_XBC]
    xbc_ref[...] = p[:, COL_XBC:COL_GLU]
    glu_ref[...] = p[:, COL_GLU:COL_DT]
    dt_ref[...] = p[:, COL_DT:PROJ_COLS]


def _in_projection(x, mod_l, n1, w_packed, qkw, bd, cos_t, sin_t, n_prompt, sample_len):
    n = x.shape[0]
    tm = TM_PROJ
    ident_blk = sample_len // tm

    def cond_row(t):
        s = t * tm
        return jnp.where(s < n_prompt, 0, 1 + (s - n_prompt) // sample_len)

    def rope_blk(t):
        s = t * tm
        return jnp.where(s < n_prompt, ident_blk, ((s - n_prompt) % sample_len) // tm)

    def rows(width):
        return pl.BlockSpec((tm, width), lambda t: (t, 0))

    widths = (ATTN_WIDTH, KV_WIDTH, KV_WIDTH, KV_WIDTH, SSD_WIDTH, SSD_XBC, 2 * CONV_WIDTH, LANES)
    dtypes = (MXU_DTYPE, MXU_DTYPE, jnp.float32, jnp.float32, jnp.float32, jnp.float32, jnp.float32,
              jnp.float32)
    return pl.pallas_call(
        _inproj_kernel,
        out_shape=[jax.ShapeDtypeStruct((n, w), dt) for w, dt in zip(widths, dtypes)],
        grid=(n // tm,),
        in_specs=[rows(D_MODEL),
                  pl.BlockSpec((1, 6, D_MODEL), lambda t: (cond_row(t), 0, 0)),
                  _full((1, D_MODEL)),
                  _full((D_MODEL, PROJ_COLS)),
                  _full((1, COL_V)),
                  _full((COL_V, COL_V)),
                  pl.BlockSpec((tm, LANES), lambda t: (rope_blk(t), 0)),
                  pl.BlockSpec((tm, LANES), lambda t: (rope_blk(t), 0))],
        out_specs=[rows(w) for w in widths],
        compiler_params=_cparams(1),
        name="in_projection",
    )(x, mod_l, n1, w_packed, qkw, bd, cos_t, sin_t)


def _attn_kernel(sink_ref, q_ref, kc_ref, vc_ref, *rest, local, blocks_per_seq):
    if local:
        kp_ref, kcur_ref, knx_ref, vp_ref, vcur_ref, vnx_ref, o_ref = rest
    else:
        (o_ref,) = rest
    blk = ATTN_BLOCK
    j = pl.program_id(0) % blocks_per_seq
    q = q_ref[...]
    row = lax.broadcasted_iota(jnp.int32, (2 * blk, 1), 0)
    if local:
        qo = lax.broadcasted_iota(jnp.int32, (2 * blk, 3 * blk), 0) % blk
        kpos = lax.broadcasted_iota(jnp.int32, (2 * blk, 3 * blk), 1) - blk
        pos = j * blk + kpos
        valid = (jnp.abs(kpos - qo) <= WINDOW) & (pos >= 0) & (pos < blocks_per_seq * blk)
    for g in range(KV_HEADS):
        sl = slice(g * HEAD_DIM, (g + 1) * HEAD_DIM)
        q2 = jnp.concatenate([q[:, (2 * g) * HEAD_DIM:(2 * g + 1) * HEAD_DIM],
                              q[:, (2 * g + 1) * HEAD_DIM:(2 * g + 2) * HEAD_DIM]], axis=0)
        s_ctx = _dot_nt(q2, _mx(kc_ref[:, sl]))
        sink = jnp.where(row < blk, sink_ref[2 * g], sink_ref[2 * g + 1])
        m = jnp.maximum(jnp.max(s_ctx, axis=-1, keepdims=True), sink)
        if local:
            kl = jnp.concatenate([_mx(kp_ref[:, sl]), _mx(kcur_ref[:, sl]), _mx(knx_ref[:, sl])], axis=0)
            vl = jnp.concatenate([_mx(vp_ref[:, sl]), _mx(vcur_ref[:, sl]), _mx(vnx_ref[:, sl])], axis=0)
            s_loc = jnp.where(valid, _dot_nt(q2, kl), -jnp.inf)
            m = jnp.maximum(m, jnp.max(s_loc, axis=-1, keepdims=True))
        p_ctx = jnp.exp(s_ctx - m)
        den = jnp.sum(p_ctx, axis=-1, keepdims=True) + jnp.exp(sink - m)
        o = _dot(_mx(p_ctx), _mx(vc_ref[:, sl]))
        if local:
            p_loc = jnp.exp(s_loc - m)
            den = den + jnp.sum(p_loc, axis=-1, keepdims=True)
            o = o + _dot(_mx(p_loc), vl)
        o = o / den
        o_ref[:, (2 * g) * HEAD_DIM:(2 * g + 1) * HEAD_DIM] = o[:blk]
        o_ref[:, (2 * g + 1) * HEAD_DIM:(2 * g + 2) * HEAD_DIM] = o[blk:]


def _attention(sink, q, k, v, kctx, vctx, *, tok_off, n_tok, seq_len, ctx_len, local):
    blk = ATTN_BLOCK
    bps = seq_len // blk
    off = tok_off // blk
    nb = n_tok // blk
    kctx_arr, kctx_off = kctx
    vctx_arr, vctx_off = vctx
    cb_k = kctx_off // ctx_len
    cb_v = vctx_off // ctx_len

    def prev_blk(i):
        return off + (i // bps) * bps + jnp.maximum(i % bps - 1, 0)

    def next_blk(i):
        return off + (i // bps) * bps + jnp.minimum(i % bps + 1, bps - 1)

    in_specs = [pl.BlockSpec(memory_space=pltpu.SMEM),
                pl.BlockSpec((blk, ATTN_WIDTH), lambda i: (off + i, 0)),
                pl.BlockSpec((ctx_len, KV_WIDTH), lambda i: (cb_k + i // bps, 0)),
                pl.BlockSpec((ctx_len, KV_WIDTH), lambda i: (cb_v + i // bps, 0))]
    args = [sink, q, kctx_arr, vctx_arr]
    if local:
        for arr in (k, v):
            in_specs += [pl.BlockSpec((blk, KV_WIDTH), lambda i: (prev_blk(i), 0)),
                         pl.BlockSpec((blk, KV_WIDTH), lambda i: (off + i, 0)),
                         pl.BlockSpec((blk, KV_WIDTH), lambda i: (next_blk(i), 0))]
            args += [arr, arr, arr]
    return pl.pallas_call(
        functools.partial(_attn_kernel, local=local, blocks_per_seq=bps),
        out_shape=jax.ShapeDtypeStruct((n_tok, ATTN_WIDTH), jnp.float32),
        grid=(nb,),
        in_specs=in_specs,
        out_specs=pl.BlockSpec((blk, ATTN_WIDTH), lambda i: (i, 0)),
        compiler_params=_cparams(1),
        name="attention_local" if local else "attention_ctx",
    )(*args)


def _ssd_local_kernel(xbc_ref, xp_ref, xn_ref, glu_ref, gp_ref, gn_ref, dt_ref,
                      cw_ref, cb_ref, dtb_ref, alog_ref, dsum_ref, ef_ref, eb_ref, tril_ref, triu_ref,
                      mw_ref, mb_ref, nw_ref, nb_ref, bd_ref,
                      y_ref, st_ref, cdec_ref, cm_ref, din_ref, u_ref,
                      xext, gext, *, n_prompt_chunks, prompt_cps, sample_cps):
    t = SSD_CHUNK
    g = pl.program_id(0)
    in_prompt = g < n_prompt_chunks
    cps = jnp.where(in_prompt, prompt_cps, sample_cps)
    cidx = jnp.where(in_prompt, g % prompt_cps, (g - n_prompt_chunks) % sample_cps)
    not_first = (cidx > 0).astype(jnp.float32)
    not_last = (cidx < cps - 1).astype(jnp.float32)

    xext[0:HALO_SSD, :] = xp_ref[...] * not_first
    xext[HALO_SSD:HALO_SSD + t, :] = xbc_ref[...]
    xext[HALO_SSD + t:HALO_SSD + t + HALO_SSD, :] = xn_ref[...] * not_last
    half = (SSD_CONV - 1) // 2
    xc_cols = []
    for cb in range(SSD_XBC // LANES):
        cols = slice(cb * LANES, (cb + 1) * LANES)
        acc = jnp.zeros((t, LANES), jnp.float32) + cb_ref[:, cols]
        for k in range(SSD_CONV):
            acc = acc + xext[pl.ds(HALO_SSD - half + k, t), cols] * cw_ref[k:k + 1, cols]
        xc_cols.append(_silu(acc))
    xc = jnp.concatenate(xc_cols, axis=1)
    xs = xc[:, :SSD_WIDTH]
    bm = xc[:, SSD_WIDTH:SSD_WIDTH + SSD_GROUPS * SSD_STATE]
    cm = xc[:, SSD_WIDTH + SSD_GROUPS * SSD_STATE:]
    cm_ref[...] = cm

    lane = lax.broadcasted_iota(jnp.int32, (t, LANES), 1)
    fwd_lane = lane < SSD_HEADS
    dt = _softplus(dt_ref[...] + dtb_ref[...])
    la = dt * (-jnp.exp(alog_ref[...]))
    cs = jnp.where(fwd_lane, _sel_dot3(tril_ref[...], la), _sel_dot3(triu_ref[...], la))
    tot = jnp.where(fwd_lane[0:1], cs[t - 1:t, :], cs[0:1, :])
    din = jnp.exp(cs)
    din_ref[...] = din
    dec = jnp.exp(tot - cs)
    cs_row = cs.T
    ef = ef_ref[...]
    eb = eb_ref[...]
    cd = jnp.broadcast_to(jnp.exp(tot), (SUBLANES, LANES))
    cdec_ref[0] = jnp.concatenate([_dot_sel2(cd, ef)[0:1], _dot_sel2(cd, eb)[0:1],
                                   jnp.zeros((SUBLANES - 2, SSD_WIDTH), jnp.float32)], axis=0)

    bm_t = bm.T
    li = lax.broadcasted_iota(jnp.int32, (t, t), 0)
    si = lax.broadcasted_iota(jnp.int32, (t, t), 1)
    scores = [_dot_nt(_mx(cm[:, gi * SSD_STATE:(gi + 1) * SSD_STATE]),
                      _mx(bm[:, gi * SSD_STATE:(gi + 1) * SSD_STATE])) for gi in range(SSD_GROUPS)]
    hpg = SSD_HEADS // SSD_GROUPS
    gw = hpg * SSD_HEAD_DIM
    y_heads = [None] * SSD_HEADS
    for d, e_d in enumerate((ef, eb)):
        xdt = xs * _dot_sel2(dt, e_d)
        xdec = _mx(xdt * _dot_sel2(dec, e_d))
        st_ref[0, d] = jnp.concatenate(
            [_dot(_mx(bm_t[gi * SSD_STATE:(gi + 1) * SSD_STATE, :]), xdec[:, gi * gw:(gi + 1) * gw])
             for gi in range(SSD_GROUPS)], axis=1)
        mask = (si <= li) if d == 0 else (si >= li)
        for h in range(SSD_HEADS):
            c = d * SSD_HEADS + h
            seg = cs[:, c:c + 1] - cs_row[c:c + 1, :]
            lmat = jnp.exp(jnp.where(mask, seg, -jnp.inf))
            yh = _dot(_mx(scores[h // hpg] * lmat), _mx(xdt[:, h * SSD_HEAD_DIM:(h + 1) * SSD_HEAD_DIM]))
            y_heads[h] = yh if y_heads[h] is None else y_heads[h] + yh
    y_ref[...] = jnp.concatenate(y_heads, axis=1) + xs * dsum_ref[...]

    def glu(v):
        return v[:, :CONV_WIDTH] * jax.nn.sigmoid(v[:, CONV_WIDTH:])

    gext[0:HALO_CM, :] = glu(gp_ref[...]) * not_first
    gext[HALO_CM:HALO_CM + t, :] = glu(glu_ref[...])
    gext[HALO_CM + t:HALO_CM + t + HALO_CM, :] = glu(gn_ref[...]) * not_last
    halfc = (CONV_KERNEL - 1) // 2
    off0 = HALO_CM - halfc
    u_cols = []
    for cb in range(CONV_WIDTH // LANES):
        cols = slice(cb * LANES, (cb + 1) * LANES)
        ucol = jnp.zeros((t, LANES), jnp.float32) + mb_ref[:, cols]
        for r in range(SUBLANES):
            taps = [k for k in range(CONV_KERNEL) if (off0 + k) % SUBLANES == r]
            part = None
            for k in taps:
                term = gext[pl.ds(off0 + k - r, t + SUBLANES), cols] * mw_ref[k:k + 1, cols]
                part = term if part is None else part + term
            if part is not None:
                ucol = ucol + part[r:r + t, :]
        u_cols.append(ucol)
    u = jnp.concatenate(u_cols, axis=1)
    gsz = CONV_WIDTH // CONV_GROUPS
    bd = bd_ref[...]
    uc = u - _dot_sel2(u, bd) * (1.0 / gsz)
    un = uc * lax.rsqrt(_dot_sel2(uc * uc, bd) * (1.0 / gsz) + EPS)
    u_ref[...] = _silu(un * nw_ref[...] + nb_ref[...])


def _ssd_local(xbc, glu, dt, pr, consts, *, n_prompt, prompt_len, sample_len):
    n = xbc.shape[0]
    t = SSD_CHUNK
    nch = n // t
    r8 = t // HALO_SSD
    r16 = t // HALO_CM
    kern = functools.partial(_ssd_local_kernel, n_prompt_chunks=n_prompt // t,
                             prompt_cps=prompt_len // t, sample_cps=sample_len // t)
    out_shape = [jax.ShapeDtypeStruct((n, SSD_WIDTH), jnp.float32),
                 jax.ShapeDtypeStruct((nch, 2, SSD_STATE, SSD_WIDTH), jnp.float32),
                 jax.ShapeDtypeStruct((nch, SUBLANES, SSD_WIDTH), jnp.float32),
                 jax.ShapeDtypeStruct((n, LANES), jnp.float32),
                 jax.ShapeDtypeStruct((n, LANES), jnp.float32),
                 jax.ShapeDtypeStruct((n, CONV_WIDTH), jnp.float32)]
    out_specs = [pl.BlockSpec((t, SSD_WIDTH), lambda g: (g, 0)),
                 pl.BlockSpec((1, 2, SSD_STATE, SSD_WIDTH), lambda g: (g, 0, 0, 0)),
                 pl.BlockSpec((1, SUBLANES, SSD_WIDTH), lambda g: (g, 0, 0)),
                 pl.BlockSpec((t, LANES), lambda g: (g, 0)),
                 pl.BlockSpec((t, LANES), lambda g: (g, 0)),
                 pl.BlockSpec((t, CONV_WIDTH), lambda g: (g, 0))]
    in_specs = [pl.BlockSpec((t, SSD_XBC), lambda g: (g, 0)),
                pl.BlockSpec((HALO_SSD, SSD_XBC), lambda g: (jnp.maximum(g * r8 - 1, 0), 0)),
                pl.BlockSpec((HALO_SSD, SSD_XBC), lambda g: (jnp.minimum((g + 1) * r8, n // HALO_SSD - 1), 0)),
                pl.BlockSpec((t, 2 * CONV_WIDTH), lambda g: (g, 0)),
                pl.BlockSpec((HALO_CM, 2 * CONV_WIDTH), lambda g: (jnp.maximum(g * r16 - 1, 0), 0)),
                pl.BlockSpec((HALO_CM, 2 * CONV_WIDTH),
                             lambda g: (jnp.minimum((g + 1) * r16, n // HALO_CM - 1), 0)),
                pl.BlockSpec((t, LANES), lambda g: (g, 0))]
    params = [pr["ssd_conv_w"], pr["ssd_conv_b"], pr["dtb"], pr["alog"], pr["dsum"],
              consts["ef"], consts["eb"], consts["tril"], consts["triu"],
              pr["cm_conv_w"], pr["cm_conv_b"], pr["cm_norm_w"], pr["cm_norm_b"], consts["bd_cm"]]
    in_specs += [_full(p.shape) for p in params]
    return pl.pallas_call(
        kern,
        out_shape=out_shape,
        grid=(nch,),
        in_specs=in_specs,
        out_specs=out_specs,
        scratch_shapes=[pltpu.VMEM((t + 2 * HALO_SSD, SSD_XBC), jnp.float32),
                        pltpu.VMEM((t + 2 * HALO_CM, CONV_WIDTH), jnp.float32)],
        compiler_params=_cparams(1),
        name="ssd_local",
    )(xbc, xbc, xbc, glu, glu, glu, dt, *params)


def _ssd_scan_kernel(seq_ref, first_ref, last_ref, stf_ref, stb_ref, cdf_ref, cdb_ref, h0f_ref, h0b_ref,
                     hsf_ref, hsb_ref, hff_ref, hfb_ref, hf, hb):
    g = pl.program_id(0)
    gb = pl.num_programs(0) - 1 - g

    @pl.when(first_ref[g] == 1)
    def _():
        hf[...] = h0f_ref[0, 0]

    @pl.when(last_ref[gb] == 1)
    def _():
        hb[...] = h0b_ref[0, 0]

    h = hf[...]
    for c in range(SCAN_CHUNKS):
        hsf_ref[c] = h
        h = h * cdf_ref[c, 0:1, :] + stf_ref[c, 0]
    hf[...] = h
    hff_ref[0, 0] = h
    h = hb[...]
    for c in reversed(range(SCAN_CHUNKS)):
        hsb_ref[c] = h
        h = h * cdb_ref[c, 1:2, :] + stb_ref[c, 0]
    hb[...] = h
    hfb_ref[0, 0] = h


def _ssd_scan(seq_id, first, last, st, cdec, h0):
    nch = st.shape[0]
    nseq = h0.shape[0]
    nstep = nch // SCAN_CHUNKS
    blk4 = (1, 1, SSD_STATE, SSD_WIDTH)
    stblk = (SCAN_CHUNKS, 1, SSD_STATE, SSD_WIDTH)
    grid_spec = pltpu.PrefetchScalarGridSpec(
        num_scalar_prefetch=3,
        grid=(nstep,),
        in_specs=[pl.BlockSpec(stblk, lambda g, s, f, l: (g, 0, 0, 0)),
                  pl.BlockSpec(stblk, lambda g, s, f, l: (nstep - 1 - g, 1, 0, 0)),
                  pl.BlockSpec((SCAN_CHUNKS, SUBLANES, SSD_WIDTH), lambda g, s, f, l: (g, 0, 0)),
                  pl.BlockSpec((SCAN_CHUNKS, SUBLANES, SSD_WIDTH), lambda g, s, f, l: (nstep - 1 - g, 0, 0)),
                  pl.BlockSpec(blk4, lambda g, s, f, l: (s[g], 0, 0, 0)),
                  pl.BlockSpec(blk4, lambda g, s, f, l: (s[nstep - 1 - g], 1, 0, 0))],
        out_specs=[pl.BlockSpec((SCAN_CHUNKS, SSD_STATE, SSD_WIDTH), lambda g, s, f, l: (g, 0, 0)),
                   pl.BlockSpec((SCAN_CHUNKS, SSD_STATE, SSD_WIDTH), lambda g, s, f, l: (nstep - 1 - g, 0, 0)),
                   pl.BlockSpec(blk4, lambda g, s, f, l: (s[g], 0, 0, 0)),
                   pl.BlockSpec(blk4, lambda g, s, f, l: (s[nstep - 1 - g], 0, 0, 0))],
        scratch_shapes=[pltpu.VMEM((SSD_STATE, SSD_WIDTH), jnp.float32),
                        pltpu.VMEM((SSD_STATE, SSD_WIDTH), jnp.float32)])
    return pl.pallas_call(
        _ssd_scan_kernel,
        out_shape=[jax.ShapeDtypeStruct((nch, SSD_STATE, SSD_WIDTH), jnp.float32),
                   jax.ShapeDtypeStruct((nch, SSD_STATE, SSD_WIDTH), jnp.float32),
                   jax.ShapeDtypeStruct((nseq, 1, SSD_STATE, SSD_WIDTH), jnp.float32),
                   jax.ShapeDtypeStruct((nseq, 1, SSD_STATE, SSD_WIDTH), jnp.float32)],
        grid_spec=grid_spec,
        compiler_params=_cparams(1),
        name="ssd_scan",
    )(seq_id, first, last, st, st, cdec, cdec, h0, h0)


def _outproj_kernel(x_ref, attn_ref, u_ref, y_ref, z_ref, cm_ref, din_ref, hsf_ref, hsb_ref, m_ref,
                    sw_ref, wo_ref, n2_ref, rwh_ref, rwl_ref, rb_ref, ef_ref, eb_ref,
                    x1_ref, xm_ref, ri_ref, rw_ref):
    for sub in range(TM_OUT // SUB_OUT):
        _outproj_subtile(sub, x_ref, attn_ref, u_ref, y_ref, z_ref, cm_ref, din_ref, hsf_ref, hsb_ref, m_ref,
                         sw_ref, wo_ref, n2_ref, rwh_ref, rwl_ref, rb_ref, ef_ref, eb_ref,
                         x1_ref, xm_ref, ri_ref, rw_ref)


def _outproj_subtile(sub, x_ref, attn_ref, u_ref, y_ref, z_ref, cm_ref, din_ref, hsf_ref, hsb_ref, m_ref,
                     sw_ref, wo_ref, n2_ref, rwh_ref, rwl_ref, rb_ref, ef_ref, eb_ref,
                     x1_ref, xm_ref, ri_ref, rw_ref):
    t = SSD_CHUNK
    hpg = SSD_HEADS // SSD_GROUPS
    gw = hpg * SSD_HEAD_DIM
    rows = slice(sub * SUB_OUT, (sub + 1) * SUB_OUT)
    ys = []
    for c in range(sub * (SUB_OUT // t), (sub + 1) * (SUB_OUT // t)):
        rs = slice(c * t, (c + 1) * t)
        cmc = cm_ref[rs, :]
        dinc = din_ref[rs, :]
        y = y_ref[rs, :]
        for hs_ref, e_ref in ((hsf_ref, ef_ref), (hsb_ref, eb_ref)):
            hs = hs_ref[c]
            yoff = jnp.concatenate(
                [_dot(_mx(cmc[:, gi * SSD_STATE:(gi + 1) * SSD_STATE]), _mx(hs[:, gi * gw:(gi + 1) * gw]))
                 for gi in range(SSD_GROUPS)], axis=1)
            y = y + yoff * _dot_sel2(dinc, e_ref[...])
        ys.append(y)
    y = jnp.concatenate(ys, axis=0) * _silu(z_ref[rows, :])
    nrm = []
    for gi in range(SSD_GROUPS):
        yg = y[:, gi * gw:(gi + 1) * gw]
        nrm.append(yg * lax.rsqrt(jnp.mean(yg * yg, axis=-1, keepdims=True) + EPS))
    y = jnp.concatenate(nrm, axis=1) * sw_ref[...]
    o1 = ATTN_WIDTH
    o2 = ATTN_WIDTH + SSD_WIDTH
    mix = (_dot(_mx(attn_ref[rows, :]), wo_ref[0:o1, :]) + _dot(_mx(y), wo_ref[o1:o2, :])
           + _dot(_mx(u_ref[rows, :]), wo_ref[o2:, :]))
    x1 = x_ref[rows, :] + m_ref[0, 2:3, :] * mix
    x1_ref[rows, :] = x1
    xm = (x1 * lax.rsqrt(jnp.mean(x1 * x1, axis=-1, keepdims=True) + EPS) * n2_ref[...]
          * (1.0 + m_ref[0, 4:5, :]) + m_ref[0, 3:4, :])
    _rows_to_tiles(xm_ref.at[rows], xm)
    xh, xl = _split2(xm)
    lg = _dot(xh, rwh_ref[...]) + _dot(xh, rwl_ref[...]) + _dot(xl, rwh_ref[...]) + rb_ref[...]
    lane_i = lax.broadcasted_iota(jnp.int32, lg.shape, 1)
    lane = lane_i.astype(jnp.float32)
    ninf = -jnp.inf
    big = float(LANES)
    is_g = lane_i < N_EXPERT_GROUPS
    gmax = jnp.max(jnp.where(is_g, lg, ninf), axis=-1, keepdims=True)
    gidx = jnp.min(jnp.where(is_g & (lg == gmax), lane, big), axis=-1, keepdims=True)
    p_grp = 1.0 / jnp.sum(jnp.where(is_g, jnp.exp(lg - gmax), 0.0), axis=-1, keepdims=True)
    lo = N_EXPERT_GROUPS + EXPERTS_PER_GROUP * gidx
    in_e = (lane >= lo) & (lane < lo + EXPERTS_PER_GROUP)
    v1 = jnp.max(jnp.where(in_e, lg, ninf), axis=-1, keepdims=True)
    i1 = jnp.min(jnp.where(in_e & (lg == v1), lane, big), axis=-1, keepdims=True)
    rest = in_e & (lane != i1)
    v2 = jnp.max(jnp.where(rest, lg, ninf), axis=-1, keepdims=True)
    i2 = jnp.min(jnp.where(rest & (lg == v2), lane, big), axis=-1, keepdims=True)
    e2 = jnp.exp(v2 - v1)
    w1 = p_grp * (1.0 / (1.0 + e2))
    w2 = p_grp * (e2 / (1.0 + e2))
    e_first = (i1 - N_EXPERT_GROUPS).astype(jnp.int32)
    e_second = (i2 - N_EXPERT_GROUPS).astype(jnp.int32)
    ri_ref[rows, :] = jnp.where(lane_i == 0, e_first, jnp.where(lane_i == 1, e_second, 0))
    rw_ref[rows, :] = jnp.where(lane_i == 0, w1, jnp.where(lane_i == 1, w2, 0.0))


def _out_projection(x, attn, u, ydiag, z, cm, din, hsf, hsb, mod_l, pr, consts, *, n_prompt, sample_len):
    n = x.shape[0]
    tm = TM_OUT
    cpt = tm // SSD_CHUNK

    def cond_row(t):
        s = t * tm
        return jnp.where(s < n_prompt, 0, 1 + (s - n_prompt) // sample_len)

    def rows(width):
        return pl.BlockSpec((tm, width), lambda t: (t, 0))

    params = [pr["ssd_norm_w"], pr["w_out"], pr["norm2"], pr["rw_hi"], pr["rw_lo"], pr["rb"],
              consts["ef"], consts["eb"]]
    return pl.pallas_call(
        _outproj_kernel,
        out_shape=[jax.ShapeDtypeStruct((n, D_MODEL), jnp.float32),
                   jax.ShapeDtypeStruct((n, ROW_TILES, LANES), jnp.float32),
                   jax.ShapeDtypeStruct((n, LANES), jnp.int32),
                   jax.ShapeDtypeStruct((n, LANES), jnp.float32)],
        grid=(n // tm,),
        in_specs=[rows(D_MODEL), rows(ATTN_WIDTH), rows(CONV_WIDTH), rows(SSD_WIDTH), rows(SSD_WIDTH),
                  rows(LANES), rows(LANES),
                  pl.BlockSpec((cpt, SSD_STATE, SSD_WIDTH), lambda t: (t, 0, 0)),
                  pl.BlockSpec((cpt, SSD_STATE, SSD_WIDTH), lambda t: (t, 0, 0)),
                  pl.BlockSpec((1, 6, D_MODEL), lambda t: (cond_row(t), 0, 0))]
        + [_full(p.shape) for p in params],
        out_specs=[rows(D_MODEL), pl.BlockSpec((tm, ROW_TILES, LANES), lambda t: (t, 0, 0)),
                   rows(LANES), rows(LANES)],
        compiler_params=_cparams(1),
        name="out_projection",
    )(x, attn, u, ydiag, z, cm, din, hsf, hsb, mod_l, *params)


def _route_slots(ridx):
    n = ridx.shape[0]
    n_assign = 2 * n
    blk = MOE_BLK
    flat = ridx[:, :2].reshape(-1)
    onehot = (flat[:, None] == jnp.arange(N_EXPERTS, dtype=jnp.int32)[None, :]).astype(jnp.int32)
    csum = jnp.cumsum(onehot, axis=0)
    counts = csum[-1]
    padded = (counts + blk - 1) // blk * blk
    pad_end = jnp.cumsum(padded)
    pad_start = pad_end - padded
    slot = jnp.sum(onehot * (csum - 1 + pad_start[None, :]), axis=1).astype(jnp.int32)
    n_blocks = n_assign // blk + N_EXPERTS
    src = jnp.zeros((n_blocks * blk,), jnp.int32).at[slot].set(jnp.arange(n_assign, dtype=jnp.int32) // 2)
    starts = jnp.arange(n_blocks, dtype=jnp.int32) * blk
    block_e = jnp.minimum(jnp.sum((pad_end[None, :] <= starts[:, None]).astype(jnp.int32), axis=1),
                          N_EXPERTS - 1).astype(jnp.int32)
    n_valid = (pad_end[-1] // blk).astype(jnp.int32).reshape(1)
    return slot, src, block_e, n_valid


def _gather_rows(idx_ref, base, src_hbm, dst, sem, n_rows):
    def body(j, carry):
        for u in range(SUBLANES):
            r = idx_ref[base + j * SUBLANES + u]
            pltpu.make_async_copy(src_hbm.at[r], dst.at[j, :, u, :], sem).start()
        return carry

    lax.fori_loop(0, n_rows // SUBLANES, body, 0)


def _gather_plain_rows(idx_ref, base, src_hbm, dst, sem, n_rows):
    def body(j, carry):
        for u in range(SUBLANES):
            jj = j * SUBLANES + u
            pltpu.make_async_copy(src_hbm.at[pl.ds(idx_ref[base + jj], 1)], dst.at[pl.ds(jj, 1)], sem).start()
        return carry

    lax.fori_loop(0, n_rows // SUBLANES, body, 0)


def _wait_rows(dst, sem):
    pltpu.make_async_copy(dst, dst, sem).wait()


def _rows_from_groups(ref, g0, g1):
    return jnp.concatenate(
        [jnp.concatenate([ref[g, s] for s in range(ROW_TILES)], axis=1) for g in range(g0, g1)], axis=0)


def _experts_kernel(be_ref, src_ref, nv_ref, x_hbm, wg_ref, wu_ref, wd_ref, o_ref,
                    xbuf, sem, wg_c, wu_c, wd_c):
    b = pl.program_id(0)
    nv = nv_ref[0]
    blk = MOE_BLK

    @pl.when(b == 0)
    def _():
        _gather_rows(src_ref, 0, x_hbm, xbuf.at[0], sem.at[0], blk)

    @pl.when(b + 1 < nv)
    def _():
        nxt = (b + 1) % 2
        _gather_rows(src_ref, (b + 1) * blk, x_hbm, xbuf.at[nxt], sem.at[nxt], blk)

    @pl.when((b == 0) | (be_ref[b] != be_ref[jnp.maximum(b - 1, 0)]))
    def _():
        wg_c[...] = _mx(wg_ref[0, 0])
        wu_c[...] = _mx(wu_ref[0, 0])
        wd_c[...] = _mx(wd_ref[0, 0])

    @pl.when(b < nv)
    def _():
        cur = b % 2
        _wait_rows(xbuf.at[cur], sem.at[cur])
        xb = _mx(_rows_from_groups(xbuf.at[cur], 0, blk // SUBLANES))
        hid = _silu(_dot(xb, wg_c[...])) * _dot(xb, wu_c[...])
        o_ref[...] = _dot(_mx(hid), wd_c[...])

    @pl.when(b >= nv)
    def _():
        o_ref[...] = jnp.zeros_like(o_ref)


def _experts(block_e, src, n_valid, xm, wg, wu, wd, layer):
    blk = MOE_BLK
    n_blocks = block_e.shape[0]
    grid_spec = pltpu.PrefetchScalarGridSpec(
        num_scalar_prefetch=3,
        grid=(n_blocks,),
        in_specs=[pl.BlockSpec(memory_space=pl.ANY),
                  pl.BlockSpec((1, 1, D_MODEL, EXPERT_FF), lambda b, be, s, nv: (layer, be[b], 0, 0)),
                  pl.BlockSpec((1, 1, D_MODEL, EXPERT_FF), lambda b, be, s, nv: (layer, be[b], 0, 0)),
                  pl.BlockSpec((1, 1, EXPERT_FF, D_MODEL), lambda b, be, s, nv: (layer, be[b], 0, 0))],
        out_specs=pl.BlockSpec((blk, D_MODEL), lambda b, be, s, nv: (b, 0)),
        scratch_shapes=[pltpu.VMEM((2, blk // SUBLANES, ROW_TILES, SUBLANES, LANES), jnp.float32),
                        pltpu.SemaphoreType.DMA((2,)),
                        pltpu.VMEM((D_MODEL, EXPERT_FF), MXU_DTYPE),
                        pltpu.VMEM((D_MODEL, EXPERT_FF), MXU_DTYPE),
                        pltpu.VMEM((EXPERT_FF, D_MODEL), MXU_DTYPE)])
    return pl.pallas_call(
        _experts_kernel,
        out_shape=jax.ShapeDtypeStruct((n_blocks * blk, D_MODEL), jnp.float32),
        grid_spec=grid_spec,
        compiler_params=_cparams(1),
        name="experts",
    )(block_e, src, n_valid, xm, wg, wu, wd)


def _combine_kernel(slot_ref, x1_ref, m_ref, rw_ref, ys_hbm, o_ref, ybuf, sem):
    i = pl.program_id(0)
    n_steps = pl.num_programs(0)
    rows = 2 * TM_COMB

    @pl.when(i == 0)
    def _():
        _gather_plain_rows(slot_ref, 0, ys_hbm, ybuf.at[0], sem.at[0], rows)

    @pl.when(i + 1 < n_steps)
    def _():
        nxt = (i + 1) % 2
        _gather_plain_rows(slot_ref, (i + 1) * rows, ys_hbm, ybuf.at[nxt], sem.at[nxt], rows)

    cur = i % 2
    _wait_rows(ybuf.at[cur], sem.at[cur])
    rw = rw_ref[...]
    ya = ybuf[cur, 0:TM_COMB, :]
    yb = ybuf[cur, TM_COMB:rows, :]
    o_ref[...] = x1_ref[...] + m_ref[0, 5:6, :] * (rw[:, 0:1] * ya + rw[:, 1:2] * yb)


def _combine(slot_km, x1, mod_l, rw, ys, *, n_prompt, sample_len):
    n = x1.shape[0]
    tm = TM_COMB

    def cond_row(t, *_):
        s = t * tm
        return jnp.where(s < n_prompt, 0, 1 + (s - n_prompt) // sample_len)

    grid_spec = pltpu.PrefetchScalarGridSpec(
        num_scalar_prefetch=1,
        grid=(n // tm,),
        in_specs=[pl.BlockSpec((tm, D_MODEL), lambda t, s: (t, 0)),
                  pl.BlockSpec((1, 6, D_MODEL), lambda t, s: (cond_row(t), 0, 0)),
                  pl.BlockSpec((tm, LANES), lambda t, s: (t, 0)),
                  pl.BlockSpec(memory_space=pl.ANY)],
        out_specs=pl.BlockSpec((tm, D_MODEL), lambda t, s: (t, 0)),
        scratch_shapes=[pltpu.VMEM((2, 2 * tm, D_MODEL), jnp.float32),
                        pltpu.SemaphoreType.DMA((2,))])
    return pl.pallas_call(
        _combine_kernel,
        out_shape=jax.ShapeDtypeStruct((n, D_MODEL), jnp.float32),
        grid_spec=grid_spec,
        compiler_params=_cparams(1),
        name="combine",
    )(slot_km, x1, mod_l, rw, ys)


def _pad_lanes(v, width=LANES):
    return jnp.pad(v, [(0, 0)] * (v.ndim - 1) + [(0, width - v.shape[-1])])


def _constants(sample_len):
    r = jnp.arange(LANES)
    c512 = jnp.arange(SSD_WIDTH)
    ef = (r[:, None] == (c512[None, :] // SSD_HEAD_DIM)).astype(MXU_DTYPE)
    eb = (r[:, None] == (c512[None, :] // SSD_HEAD_DIM + SSD_HEADS)).astype(MXU_DTYPE)
    t = jnp.arange(SSD_CHUNK)
    tril = (t[None, :] <= t[:, None]).astype(MXU_DTYPE)
    triu = (t[None, :] >= t[:, None]).astype(MXU_DTYPE)
    qk = jnp.arange(COL_V)
    bd_qk = (qk[:, None] // HEAD_DIM == qk[None, :] // HEAD_DIM).astype(MXU_DTYPE)
    cw = jnp.arange(CONV_WIDTH)
    gsz = CONV_WIDTH // CONV_GROUPS
    bd_cm = (cw[:, None] // gsz == cw[None, :] // gsz).astype(MXU_DTYPE)
    pos = jnp.arange(sample_len)
    row = (pos // GRID_W).astype(jnp.float32)
    col = (pos % GRID_W).astype(jnp.float32)
    n_freq = HEAD_DIM // 4
    inv = ROPE_THETA ** (-jnp.arange(n_freq, dtype=jnp.float32) / n_freq)
    ar = row[:, None] * inv
    ac = col[:, None] * inv
    cos64 = jnp.concatenate([jnp.cos(ar), jnp.cos(ar), jnp.cos(ac), jnp.cos(ac)], axis=1)
    sin64 = jnp.concatenate([-jnp.sin(ar), jnp.sin(ar), -jnp.sin(ac), jnp.sin(ac)], axis=1)
    cos_t = jnp.concatenate([jnp.tile(cos64, (1, 2)), jnp.ones((TM_PROJ, LANES), jnp.float32)], axis=0)
    sin_t = jnp.concatenate([jnp.tile(sin64, (1, 2)), jnp.zeros((TM_PROJ, LANES), jnp.float32)], axis=0)
    return dict(ef=ef, eb=eb, tril=tril, triu=triu, bd_qk=bd_qk, bd_cm=bd_cm, cos=cos_t, sin=sin_t)


def _pack_w_in(w_in):
    o = 0
    parts = {}
    for name, size in (("q", ATTN_WIDTH), ("k", KV_WIDTH), ("v", KV_WIDTH), ("z", SSD_WIDTH),
                       ("xbc", SSD_XBC), ("dt", SSD_HEADS), ("glu", 2 * CONV_WIDTH)):
        parts[name] = w_in[:, o:o + size]
        o += size
    dt2 = _pad_lanes(jnp.concatenate([parts["dt"], parts["dt"]], axis=1))
    return _mx(jnp.concatenate([parts["q"], parts["k"], parts["v"], parts["z"], parts["xbc"],
                                parts["glu"], dt2], axis=1))


def _layer_params(l, norm1_w, norm2_w, w_in, q_norm_w, k_norm_w, attn_sink, ssd_conv_w, ssd_conv_b,
                  ssd_dt_bias, ssd_a_log, ssd_d, ssd_norm_w, cm_conv_w, cm_conv_b, cm_norm_w, cm_norm_b,
                  w_out, router_group_w, router_group_b, router_expert_w, router_expert_b):
    rw = _pad_lanes(jnp.concatenate([router_group_w[l], router_expert_w[l]], axis=1))
    rw_hi = _mx(rw)
    rw_lo = _mx(rw - _f32(rw_hi))
    return dict(
        norm1=norm1_w[l][None, :],
        norm2=norm2_w[l][None, :],
        w_in=_pack_w_in(w_in[l]),
        qkw=jnp.concatenate([jnp.tile(q_norm_w[l], ATTN_HEADS), jnp.tile(k_norm_w[l], KV_HEADS)])[None, :],
        sink=attn_sink[l],
        ssd_conv_w=ssd_conv_w[l],
        ssd_conv_b=ssd_conv_b[l][None, :],
        dtb=_pad_lanes(ssd_dt_bias[l].reshape(1, 2 * SSD_HEADS)),
        alog=_pad_lanes(ssd_a_log[l].reshape(1, 2 * SSD_HEADS)),
        dsum=jnp.repeat(ssd_d[l, 0] + ssd_d[l, 1], SSD_HEAD_DIM)[None, :],
        ssd_norm_w=ssd_norm_w[l][None, :],
        cm_conv_w=cm_conv_w[l],
        cm_conv_b=cm_conv_b[l][None, :],
        cm_norm_w=cm_norm_w[l][None, :],
        cm_norm_b=cm_norm_b[l][None, :],
        w_out=_mx(w_out[l]),
        rw_hi=rw_hi,
        rw_lo=rw_lo,
        rb=_pad_lanes(jnp.concatenate([router_group_b[l], router_expert_b[l]])[None, :]),
    )


def _chunk_tables(n_prompt_seq, prompt_len, n_sample_seq, sample_len):
    t = SSD_CHUNK * SCAN_CHUNKS
    assert prompt_len % t == 0 and sample_len % t == 0
    pc, sc = prompt_len // t, sample_len // t
    seq = [s for s in range(n_prompt_seq) for _ in range(pc)]
    seq += [n_prompt_seq + s for s in range(n_sample_seq) for _ in range(sc)]
    first = [int(i == 0) for _ in range(n_prompt_seq) for i in range(pc)]
    first += [int(i == 0) for _ in range(n_sample_seq) for i in range(sc)]
    last = [int(i == pc - 1) for _ in range(n_prompt_seq) for i in range(pc)]
    last += [int(i == sc - 1) for _ in range(n_sample_seq) for i in range(sc)]
    return (jnp.asarray(seq, jnp.int32), jnp.asarray(first, jnp.int32), jnp.asarray(last, jnp.int32))


def _layer(x, mod_l, pr, consts, tables, ck, cv, h0, wg, wu, wd, layer, *, n_prompt, prompt_len, sample_len):
    n = x.shape[0]
    n_sample = n - n_prompt
    q, kr, kn, v, z, xbc, glu, dt = _in_projection(
        x, mod_l, pr["norm1"], pr["w_in"], pr["qkw"], consts["bd_qk"], consts["cos"], consts["sin"],
        n_prompt, sample_len)
    attn_p = _attention(pr["sink"], q, kr, v, (kr, 0), (v, 0), tok_off=0, n_tok=n_prompt,
                        seq_len=prompt_len, ctx_len=prompt_len, local=False)
    attn_s = _attention(pr["sink"], q, kr, v, (ck, 0), (cv, 0), tok_off=n_prompt, n_tok=n_sample,
                        seq_len=sample_len, ctx_len=ck.shape[0] // (n_sample // sample_len), local=True)
    attn = jnp.concatenate([attn_p, attn_s], axis=0)
    ydiag, st, cdec, cm, din, u = _ssd_local(xbc, glu, dt, pr, consts, n_prompt=n_prompt,
                                             prompt_len=prompt_len, sample_len=sample_len)
    hsf, hsb, hff, hfb = _ssd_scan(*tables, st, cdec, h0)
    x1, xm, ridx, rwt = _out_projection(x, attn, u, ydiag, z, cm, din, hsf, hsb, mod_l, pr, consts,
                                        n_prompt=n_prompt, sample_len=sample_len)
    slot, src, block_e, n_valid = _route_slots(ridx)
    ys = _experts(block_e, src, n_valid, xm, wg, wu, wd, layer)
    slot_km = slot.reshape(n // TM_COMB, TM_COMB, 2).transpose(0, 2, 1).reshape(-1)
    x2 = _combine(slot_km, x1, mod_l, rwt, ys, n_prompt=n_prompt, sample_len=sample_len)
    return x2, kn, v, hff, hfb


def kernel(x_prompt, x_sample, c, cache_k, cache_v, state_ssm, c_ctx, mod_w, mod_b, norm1_w, norm2_w, w_in, q_norm_w, k_norm_w, attn_sink, ssd_conv_w, ssd_conv_b, ssd_dt_bias, ssd_a_log, ssd_d, ssd_norm_w, cm_conv_w, cm_conv_b, cm_norm_w, cm_norm_b, w_out, router_group_w, router_group_b, router_expert_w, router_expert_b, expert_w_gate, expert_w_up, expert_w_down):
    bp, lp, d = x_prompt.shape
    bs, ls, _ = x_sample.shape
    depth = mod_w.shape[0]
    past = cache_k.shape[2]
    n_prompt = bp * lp
    assert d == D_MODEL and bs + 1 <= SUBLANES
    assert all(v % tm == 0 for v in (n_prompt, ls) for tm in (TM_OUT, TM_PROJ, TM_COMB))
    assert lp % SSD_CHUNK == 0 and ls % SSD_CHUNK == 0

    cond8 = jnp.concatenate([c_ctx[None, :], c, jnp.zeros((SUBLANES - 1 - bs, d), jnp.float32)], axis=0)
    mod = _modulation(cond8, mod_w, mod_b).reshape(depth, SUBLANES, 6, d)
    consts = _constants(ls)
    tables = _chunk_tables(bp, lp, bs, ls)

    x = jnp.concatenate([x_prompt.reshape(n_prompt, d), x_sample.reshape(bs * ls, d)], axis=0)
    ks, vs, sts = [], [], []
    for l in range(depth):
        pr = _layer_params(l, norm1_w, norm2_w, w_in, q_norm_w, k_norm_w, attn_sink, ssd_conv_w, ssd_conv_b,
                           ssd_dt_bias, ssd_a_log, ssd_d, ssd_norm_w, cm_conv_w, cm_conv_b, cm_norm_w,
                           cm_norm_b, w_out, router_group_w, router_group_b, router_expert_w,
                           router_expert_b)
        ck = cache_k[:, l].reshape(bs * past, KV_WIDTH)
        cv = cache_v[:, l].reshape(bs * past, KV_WIDTH)
        h0s = state_ssm[:, l].transpose(0, 1, 4, 2, 3).reshape(bs, 2, SSD_STATE, SSD_WIDTH)
        h0 = jnp.concatenate([jnp.zeros((bp, 2, SSD_STATE, SSD_WIDTH), jnp.float32), h0s], axis=0)
        x, kn, v, hff, hfb = _layer(x, mod[l], pr, consts, tables, ck, cv, h0,
                                    expert_w_gate, expert_w_up, expert_w_down, l,
                                    n_prompt=n_prompt, prompt_len=lp, sample_len=ls)
        ks.append(kn[:n_prompt].reshape(bp, lp, KV_HEADS, HEAD_DIM))
        vs.append(v[:n_prompt].reshape(bp, lp, KV_HEADS, HEAD_DIM))
        hfin = jnp.concatenate([hff[:bp], hfb[:bp]], axis=1)
        sts.append(hfin.reshape(bp, 2, SSD_STATE, SSD_HEADS, SSD_HEAD_DIM).transpose(0, 1, 3, 4, 2))
    y_prompt = x[:n_prompt].reshape(bp, lp, d)
    y_sample = x[n_prompt:].reshape(bs, ls, d)
    return (y_prompt, y_sample, jnp.stack(ks, axis=1), jnp.stack(vs, axis=1), jnp.stack(sts, axis=1))
```

```python
import functools
import math

import jax
import jax.numpy as jnp
from jax import lax
from jax.experimental import pallas as pl
from jax.experimental.pallas import tpu as pltpu

D_MODEL = 1024
GRID_W = 64
HEAD_DIM = 64
ATTN_HEADS = 4
KV_HEADS = 2
ATTN_WIDTH = ATTN_HEADS * HEAD_DIM
KV_WIDTH = KV_HEADS * HEAD_DIM
WINDOW = 128
ATTN_BLOCK = 128
ROPE_THETA = 10000.0
SSD_HEAD_DIM = 64
SSD_WIDTH = 512
SSD_HEADS = 8
SSD_GROUPS = 2
SSD_STATE = 64
SSD_CONV = 5
SSD_CHUNK = 128
SSD_XBC = SSD_WIDTH + 2 * SSD_GROUPS * SSD_STATE
CONV_WIDTH = 256
CONV_KERNEL = 31
CONV_GROUPS = 4
N_EXPERT_GROUPS = 4
EXPERTS_PER_GROUP = 8
N_EXPERTS = 32
EXPERT_FF = 256
EPS = 1e-6

LANES = 128
SUBLANES = 8
VMEM_LIMIT = 52 * 1024 * 1024

MXU_DTYPE = jnp.bfloat16
TM_PROJ = 512
TM_OUT = 4 * SSD_CHUNK
SUB_OUT = 2 * SSD_CHUNK
SCAN_CHUNKS = 2
MOE_BLK = 256
TM_COMB = 256
MOD_TN = 1536
HALO_SSD = 8
HALO_CM = 16
ROW_TILES = D_MODEL // LANES
COL_Q = 0
COL_K = ATTN_WIDTH
COL_V = COL_K + KV_WIDTH
COL_Z = COL_V + KV_WIDTH
COL_XBC = COL_Z + SSD_WIDTH
COL_GLU = COL_XBC + SSD_XBC
COL_DT = COL_GLU + 2 * CONV_WIDTH
PROJ_COLS = COL_DT + LANES


def _f32(x):
    return x.astype(jnp.float32)


def _mx(x):
    return x.astype(MXU_DTYPE)


def _dot(a, b):
    return jnp.dot(a, b, preferred_element_type=jnp.float32)


def _dot_nt(a, b):
    return lax.dot_general(a, b, (((1,), (1,)), ((), ())), preferred_element_type=jnp.float32)


def _split2(x):
    hi = _mx(x)
    lo = _mx(x - _f32(hi))
    return hi, lo


def _split3(x):
    hi = _mx(x)
    r = x - _f32(hi)
    mid = _mx(r)
    lo = _mx(r - _f32(mid))
    return hi, mid, lo


def _dot_sel2(x, sel):
    hi, lo = _split2(x)
    return _dot(hi, sel) + _dot(lo, sel)


def _sel_dot3(sel, x):
    hi, mid, lo = _split3(x)
    return _dot(sel, hi) + _dot(sel, mid) + _dot(sel, lo)


def _silu(x):
    return x * jax.nn.sigmoid(x)


def _softplus(x):
    return jnp.maximum(x, 0.0) + jnp.log(1.0 + jnp.exp(-jnp.abs(x)))


def _rows_to_tiles(ref, val):
    for s in range(ROW_TILES):
        ref[:, s, :] = val[:, s * LANES:(s + 1) * LANES]


def _cparams(n_axes=1):
    return pltpu.CompilerParams(dimension_semantics=("arbitrary",) * n_axes,
                                vmem_limit_bytes=VMEM_LIMIT)


def _full(shape):
    nd = len(shape)
    return pl.BlockSpec(shape, lambda *_: (0,) * nd)


def _mod_kernel(c_ref, w_ref, b_ref, o_ref):
    s = _silu(c_ref[...])
    o_ref[0] = _dot(_mx(s), _mx(w_ref[0])) + b_ref[0]


def _modulation(cond8, mod_w, mod_b):
    depth, d, n6 = mod_w.shape
    return pl.pallas_call(
        _mod_kernel,
        out_shape=jax.ShapeDtypeStruct((depth, SUBLANES, n6), jnp.float32),
        grid=(depth, n6 // MOD_TN),
        in_specs=[pl.BlockSpec((SUBLANES, d), lambda l, j: (0, 0)),
                  pl.BlockSpec((1, d, MOD_TN), lambda l, j: (l, 0, j)),
                  pl.BlockSpec((1, 1, MOD_TN), lambda l, j: (l, 0, j))],
        out_specs=pl.BlockSpec((1, SUBLANES, MOD_TN), lambda l, j: (l, 0, j)),
        compiler_params=_cparams(2),
        name="modulation",
    )(cond8, mod_w, mod_b.reshape(depth, 1, n6))


def _rope128(x, cos, sin, first_half):
    partner = jnp.where(first_half, pltpu.roll(x, LANES - 16, 1), pltpu.roll(x, 16, 1))
    return x * cos + partner * sin


def _inproj_kernel(x_ref, m_ref, n1_ref, w_ref, qkw_ref, bd_ref, cos_ref, sin_ref,
                   q_ref, kr_ref, kn_ref, v_ref, z_ref, xbc_ref, glu_ref, dt_ref):
    x = x_ref[...]
    xn = x * lax.rsqrt(jnp.mean(x * x, axis=-1, keepdims=True) + EPS) * n1_ref[...]
    h = xn * (1.0 + m_ref[0, 1:2, :]) + m_ref[0, 0:1, :]
    p = _dot(_mx(h), w_ref[...])
    qk = p[:, COL_Q:COL_V]
    seg = _dot_sel2(qk * qk, bd_ref[...])
    qk = qk * lax.rsqrt(seg * (1.0 / HEAD_DIM) + EPS) * qkw_ref[...]
    cos = cos_ref[...]
    sin = sin_ref[...]
    lane = lax.broadcasted_iota(jnp.int32, cos.shape, 1)
    first_half = (lane % 32) < 16
    scale = HEAD_DIM ** -0.5
    for c in range(ATTN_WIDTH // LANES):
        qc = _rope128(qk[:, c * LANES:(c + 1) * LANES], cos, sin, first_half)
        q_ref[:, c * LANES:(c + 1) * LANES] = (qc * scale).astype(q_ref.dtype)
    kn = qk[:, COL_K:COL_V]
    kn_ref[...] = kn
    kr_ref[...] = _rope128(kn, cos, sin, first_half).astype(kr_ref.dtype)
    v_ref[...] = p[:, COL_V:COL_Z]
    z_ref[...] = p[:, COL_Z:COL_XBC]
    xbc_ref[...] = p[:, COL_XBC:COL_GLU]
    glu_ref[...] = p[:, COL_GLU:COL_DT]
    dt_ref[...] = p[:, COL_DT:PROJ_COLS]


def _in_projection(x, mod_l, n1, w_packed, qkw, bd, cos_t, sin_t, n_prompt, sample_len):
    n = x.shape[0]
    tm = TM_PROJ
    ident_blk = sample_len // tm

    def cond_row(t):
        s = t * tm
        return jnp.where(s < n_prompt, 0, 1 + (s - n_prompt) // sample_len)

    def rope_blk(t):
        s = t * tm
        return jnp.where(s < n_prompt, ident_blk, ((s - n_prompt) % sample_len) // tm)

    def rows(width):
        return pl.BlockSpec((tm, width), lambda t: (t, 0))

    widths = (ATTN_WIDTH, KV_WIDTH, KV_WIDTH, KV_WIDTH, SSD_WIDTH, SSD_XBC, 2 * CONV_WIDTH, LANES)
    dtypes = (MXU_DTYPE, MXU_DTYPE, jnp.float32, jnp.float32, jnp.float32, jnp.float32, jnp.float32,
              jnp.float32)
    return pl.pallas_call(
        _inproj_kernel,
        out_shape=[jax.ShapeDtypeStruct((n, w), dt) for w, dt in zip(widths, dtypes)],
        grid=(n // tm,),
        in_specs=[rows(D_MODEL),
                  pl.BlockSpec((1, 6, D_MODEL), lambda t: (cond_row(t), 0, 0)),
                  _full((1, D_MODEL)),
                  _full((D_MODEL, PROJ_COLS)),
                  _full((1, COL_V)),
                  _full((COL_V, COL_V)),
                  pl.BlockSpec((tm, LANES), lambda t: (rope_blk(t), 0)),
                  pl.BlockSpec((tm, LANES), lambda t: (rope_blk(t), 0))],
        out_specs=[rows(w) for w in widths],
        compiler_params=_cparams(1),
        name="in_projection",
    )(x, mod_l, n1, w_packed, qkw, bd, cos_t, sin_t)


def _attn_kernel(sink_ref, q_ref, kc_ref, vc_ref, *rest, local, blocks_per_seq):
    if local:
        kp_ref, kcur_ref, knx_ref, vp_ref, vcur_ref, vnx_ref, o_ref = rest
    else:
        (o_ref,) = rest
    blk = ATTN_BLOCK
    j = pl.program_id(0) % blocks_per_seq
    q = q_ref[...]
    row = lax.broadcasted_iota(jnp.int32, (2 * blk, 1), 0)
    if local:
        qo = lax.broadcasted_iota(jnp.int32, (2 * blk, 3 * blk), 0) % blk
        kpos = lax.broadcasted_iota(jnp.int32, (2 * blk, 3 * blk), 1) - blk
        pos = j * blk + kpos
        valid = (jnp.abs(kpos - qo) <= WINDOW) & (pos >= 0) & (pos < blocks_per_seq * blk)
    for g in range(KV_HEADS):
        sl = slice(g * HEAD_DIM, (g + 1) * HEAD_DIM)
        q2 = jnp.concatenate([q[:, (2 * g) * HEAD_DIM:(2 * g + 1) * HEAD_DIM],
                              q[:, (2 * g + 1) * HEAD_DIM:(2 * g + 2) * HEAD_DIM]], axis=0)
        s_ctx = _dot_nt(q2, _mx(kc_ref[:, sl]))
        sink = jnp.where(row < blk, sink_ref[2 * g], sink_ref[2 * g + 1])
        m = jnp.maximum(jnp.max(s_ctx, axis=-1, keepdims=True), sink)
        if local:
            kl = jnp.concatenate([_mx(kp_ref[:, sl]), _mx(kcur_ref[:, sl]), _mx(knx_ref[:, sl])], axis=0)
            vl = jnp.concatenate([_mx(vp_ref[:, sl]), _mx(vcur_ref[:, sl]), _mx(vnx_ref[:, sl])], axis=0)
            s_loc = jnp.where(valid, _dot_nt(q2, kl), -jnp.inf)
            m = jnp.maximum(m, jnp.max(s_loc, axis=-1, keepdims=True))
        p_ctx = jnp.exp(s_ctx - m)
        den = jnp.sum(p_ctx, axis=-1, keepdims=True) + jnp.exp(sink - m)
        o = _dot(_mx(p_ctx), _mx(vc_ref[:, sl]))
        if local:
            p_loc = jnp.exp(s_loc - m)
            den = den + jnp.sum(p_loc, axis=-1, keepdims=True)
            o = o + _dot(_mx(p_loc), vl)
        o = o / den
        o_ref[:, (2 * g) * HEAD_DIM:(2 * g + 1) * HEAD_DIM] = o[:blk]
        o_ref[:, (2 * g + 1) * HEAD_DIM:(2 * g + 2) * HEAD_DIM] = o[blk:]


def _attention(sink, q, k, v, kctx, vctx, *, tok_off, n_tok, seq_len, ctx_len, local):
    blk = ATTN_BLOCK
    bps = seq_len // blk
    off = tok_off // blk
    nb = n_tok // blk
    kctx_arr, kctx_off = kctx
    vctx_arr, vctx_off = vctx
    cb_k = kctx_off // ctx_len
    cb_v = vctx_off // ctx_len

    def prev_blk(i):
        return off + (i // bps) * bps + jnp.maximum(i % bps - 1, 0)

    def next_blk(i):
        return off + (i // bps) * bps + jnp.minimum(i % bps + 1, bps - 1)

    in_specs = [pl.BlockSpec(memory_space=pltpu.SMEM),
                pl.BlockSpec((blk, ATTN_WIDTH), lambda i: (off + i, 0)),
                pl.BlockSpec((ctx_len, KV_WIDTH), lambda i: (cb_k + i // bps, 0)),
                pl.BlockSpec((ctx_len, KV_WIDTH), lambda i: (cb_v + i // bps, 0))]
    args = [sink, q, kctx_arr, vctx_arr]
    if local:
        for arr in (k, v):
            in_specs += [pl.BlockSpec((blk, KV_WIDTH), lambda i: (prev_blk(i), 0)),
                         pl.BlockSpec((blk, KV_WIDTH), lambda i: (off + i, 0)),
                         pl.BlockSpec((blk, KV_WIDTH), lambda i: (next_blk(i), 0))]
            args += [arr, arr, arr]
    return pl.pallas_call(
        functools.partial(_attn_kernel, local=local, blocks_per_seq=bps),
        out_shape=jax.ShapeDtypeStruct((n_tok, ATTN_WIDTH), jnp.float32),
        grid=(nb,),
        in_specs=in_specs,
        out_specs=pl.BlockSpec((blk, ATTN_WIDTH), lambda i: (i, 0)),
        compiler_params=_cparams(1),
        name="attention_local" if local else "attention_ctx",
    )(*args)


def _ssd_local_kernel(xbc_ref, xp_ref, xn_ref, glu_ref, gp_ref, gn_ref, dt_ref,
                      cw_ref, cb_ref, dtb_ref, alog_ref, dsum_ref, ef_ref, eb_ref, tril_ref, triu_ref,
                      mw_ref, mb_ref, nw_ref, nb_ref, bd_ref,
                      y_ref, st_ref, cdec_ref, cm_ref, din_ref, u_ref,
                      xext, gext, *, n_prompt_chunks, prompt_cps, sample_cps):
    t = SSD_CHUNK
    g = pl.program_id(0)
    in_prompt = g < n_prompt_chunks
    cps = jnp.where(in_prompt, prompt_cps, sample_cps)
    cidx = jnp.where(in_prompt, g % prompt_cps, (g - n_prompt_chunks) % sample_cps)
    not_first = (cidx > 0).astype(jnp.float32)
    not_last = (cidx < cps - 1).astype(jnp.float32)

    xext[0:HALO_SSD, :] = xp_ref[...] * not_first
    xext[HALO_SSD:HALO_SSD + t, :] = xbc_ref[...]
    xext[HALO_SSD + t:HALO_SSD + t + HALO_SSD, :] = xn_ref[...] * not_last
    half = (SSD_CONV - 1) // 2
    xc_cols = []
    for cb in range(SSD_XBC // LANES):
        cols = slice(cb * LANES, (cb + 1) * LANES)
        acc = jnp.zeros((t, LANES), jnp.float32) + cb_ref[:, cols]
        for k in range(SSD_CONV):
            acc = acc + xext[pl.ds(HALO_SSD - half + k, t), cols] * cw_ref[k:k + 1, cols]
        xc_cols.append(_silu(acc))
    xc = jnp.concatenate(xc_cols, axis=1)
    xs = xc[:, :SSD_WIDTH]
    bm = xc[:, SSD_WIDTH:SSD_WIDTH + SSD_GROUPS * SSD_STATE]
    cm = xc[:, SSD_WIDTH + SSD_GROUPS * SSD_STATE:]
    cm_ref[...] = cm

    lane = lax.broadcasted_iota(jnp.int32, (t, LANES), 1)
    fwd_lane = lane < SSD_HEADS
    dt = _softplus(dt_ref[...] + dtb_ref[...])
    la = dt * (-jnp.exp(alog_ref[...]))
    cs = jnp.where(fwd_lane, _sel_dot3(tril_ref[...], la), _sel_dot3(triu_ref[...], la))
    tot = jnp.where(fwd_lane[0:1], cs[t - 1:t, :], cs[0:1, :])
    din = jnp.exp(cs)
    din_ref[...] = din
    dec = jnp.exp(tot - cs)
    cs_row = cs.T
    ef = ef_ref[...]
    eb = eb_ref[...]
    cd = jnp.broadcast_to(jnp.exp(tot), (SUBLANES, LANES))
    cdec_ref[0] = jnp.concatenate([_dot_sel2(cd, ef)[0:1], _dot_sel2(cd, eb)[0:1],
                                   jnp.zeros((SUBLANES - 2, SSD_WIDTH), jnp.float32)], axis=0)

    bm_t = bm.T
    li = lax.broadcasted_iota(jnp.int32, (t, t), 0)
    si = lax.broadcasted_iota(jnp.int32, (t, t), 1)
    scores = [_dot_nt(_mx(cm[:, gi * SSD_STATE:(gi + 1) * SSD_STATE]),
                      _mx(bm[:, gi * SSD_STATE:(gi + 1) * SSD_STATE])) for gi in range(SSD_GROUPS)]
    hpg = SSD_HEADS // SSD_GROUPS
    gw = hpg * SSD_HEAD_DIM
    y_heads = [None] * SSD_HEADS
    for d, e_d in enumerate((ef, eb)):
        xdt = xs * _dot_sel2(dt, e_d)
        xdec = _mx(xdt * _dot_sel2(dec, e_d))
        st_ref[0, d] = jnp.concatenate(
            [_dot(_mx(bm_t[gi * SSD_STATE:(gi + 1) * SSD_STATE, :]), xdec[:, gi * gw:(gi + 1) * gw])
             for gi in range(SSD_GROUPS)], axis=1)
        mask = (si <= li) if d == 0 else (si >= li)
        for h in range(SSD_HEADS):
            c = d * SSD_HEADS + h
            seg = cs[:, c:c + 1] - cs_row[c:c + 1, :]
            lmat = jnp.exp(jnp.where(mask, seg, -jnp.inf))
            yh = _dot(_mx(scores[h // hpg] * lmat), _mx(xdt[:, h * SSD_HEAD_DIM:(h + 1) * SSD_HEAD_DIM]))
            y_heads[h] = yh if y_heads[h] is None else y_heads[h] + yh
    y_ref[...] = jnp.concatenate(y_heads, axis=1) + xs * dsum_ref[...]

    def glu(v):
        return v[:, :CONV_WIDTH] * jax.nn.sigmoid(v[:, CONV_WIDTH:])

    gext[0:HALO_CM, :] = glu(gp_ref[...]) * not_first
    gext[HALO_CM:HALO_CM + t, :] = glu(glu_ref[...])
    gext[HALO_CM + t:HALO_CM + t + HALO_CM, :] = glu(gn_ref[...]) * not_last
    halfc = (CONV_KERNEL - 1) // 2
    off0 = HALO_CM - halfc
    u_cols = []
    for cb in range(CONV_WIDTH // LANES):
        cols = slice(cb * LANES, (cb + 1) * LANES)
        ucol = jnp.zeros((t, LANES), jnp.float32) + mb_ref[:, cols]
        for r in range(SUBLANES):
            taps = [k for k in range(CONV_KERNEL) if (off0 + k) % SUBLANES == r]
            part = None
            for k in taps:
                term = gext[pl.ds(off0 + k - r, t + SUBLANES), cols] * mw_ref[k:k + 1, cols]
                part = term if part is None else part + term
            if part is not None:
                ucol = ucol + part[r:r + t, :]
        u_cols.append(ucol)
    u = jnp.concatenate(u_cols, axis=1)
    gsz = CONV_WIDTH // CONV_GROUPS
    bd = bd_ref[...]
    uc = u - _dot_sel2(u, bd) * (1.0 / gsz)
    un = uc * lax.rsqrt(_dot_sel2(uc * uc, bd) * (1.0 / gsz) + EPS)
    u_ref[...] = _silu(un * nw_ref[...] + nb_ref[...])


def _ssd_local(xbc, glu, dt, pr, consts, *, n_prompt, prompt_len, sample_len):
    n = xbc.shape[0]
    t = SSD_CHUNK
    nch = n // t
    r8 = t // HALO_SSD
    r16 = t // HALO_CM
    kern = functools.partial(_ssd_local_kernel, n_prompt_chunks=n_prompt // t,
                             prompt_cps=prompt_len // t, sample_cps=sample_len // t)
    out_shape = [jax.ShapeDtypeStruct((n, SSD_WIDTH), jnp.float32),
                 jax.ShapeDtypeStruct((nch, 2, SSD_STATE, SSD_WIDTH), jnp.float32),
                 jax.ShapeDtypeStruct((nch, SUBLANES, SSD_WIDTH), jnp.float32),
                 jax.ShapeDtypeStruct((n, LANES), jnp.float32),
                 jax.ShapeDtypeStruct((n, LANES), jnp.float32),
                 jax.ShapeDtypeStruct((n, CONV_WIDTH), jnp.float32)]
    out_specs = [pl.BlockSpec((t, SSD_WIDTH), lambda g: (g, 0)),
                 pl.BlockSpec((1, 2, SSD_STATE, SSD_WIDTH), lambda g: (g, 0, 0, 0)),
                 pl.BlockSpec((1, SUBLANES, SSD_WIDTH), lambda g: (g, 0, 0)),
                 pl.BlockSpec((t, LANES), lambda g: (g, 0)),
                 pl.BlockSpec((t, LANES), lambda g: (g, 0)),
                 pl.BlockSpec((t, CONV_WIDTH), lambda g: (g, 0))]
    in_specs = [pl.BlockSpec((t, SSD_XBC), lambda g: (g, 0)),
                pl.BlockSpec((HALO_SSD, SSD_XBC), lambda g: (jnp.maximum(g * r8 - 1, 0), 0)),
                pl.BlockSpec((HALO_SSD, SSD_XBC), lambda g: (jnp.minimum((g + 1) * r8, n // HALO_SSD - 1), 0)),
                pl.BlockSpec((t, 2 * CONV_WIDTH), lambda g: (g, 0)),
                pl.BlockSpec((HALO_CM, 2 * CONV_WIDTH), lambda g: (jnp.maximum(g * r16 - 1, 0), 0)),
                pl.BlockSpec((HALO_CM, 2 * CONV_WIDTH),
                             lambda g: (jnp.minimum((g + 1) * r16, n // HALO_CM - 1), 0)),
                pl.BlockSpec((t, LANES), lambda g: (g, 0))]
    params = [pr["ssd_conv_w"], pr["ssd_conv_b"], pr["dtb"], pr["alog"], pr["dsum"],
              consts["ef"], consts["eb"], consts["tril"], consts["triu"],
              pr["cm_conv_w"], pr["cm_conv_b"], pr["cm_norm_w"], pr["cm_norm_b"], consts["bd_cm"]]
    in_specs += [_full(p.shape) for p in params]
    return pl.pallas_call(
        kern,
        out_shape=out_shape,
        grid=(nch,),
        in_specs=in_specs,
        out_specs=out_specs,
        scratch_shapes=[pltpu.VMEM((t + 2 * HALO_SSD, SSD_XBC), jnp.float32),
                        pltpu.VMEM((t + 2 * HALO_CM, CONV_WIDTH), jnp.float32)],
        compiler_params=_cparams(1),
        name="ssd_local",
    )(xbc, xbc, xbc, glu, glu, glu, dt, *params)


def _ssd_scan_kernel(seq_ref, first_ref, last_ref, stf_ref, stb_ref, cdf_ref, cdb_ref, h0f_ref, h0b_ref,
                     hsf_ref, hsb_ref, hff_ref, hfb_ref, hf, hb):
    g = pl.program_id(0)
    gb = pl.num_programs(0) - 1 - g

    @pl.when(first_ref[g] == 1)
    def _():
        hf[...] = h0f_ref[0, 0]

    @pl.when(last_ref[gb] == 1)
    def _():
        hb[...] = h0b_ref[0, 0]

    h = hf[...]
    for c in range(SCAN_CHUNKS):
        hsf_ref[c] = h
        h = h * cdf_ref[c, 0:1, :] + stf_ref[c, 0]
    hf[...] = h
    hff_ref[0, 0] = h
    h = hb[...]
    for c in reversed(range(SCAN_CHUNKS)):
        hsb_ref[c] = h
        h = h * cdb_ref[c, 1:2, :] + stb_ref[c, 0]
    hb[...] = h
    hfb_ref[0, 0] = h


def _ssd_scan(seq_id, first, last, st, cdec, h0):
    nch = st.shape[0]
    nseq = h0.shape[0]
    nstep = nch // SCAN_CHUNKS
    blk4 = (1, 1, SSD_STATE, SSD_WIDTH)
    stblk = (SCAN_CHUNKS, 1, SSD_STATE, SSD_WIDTH)
    grid_spec = pltpu.PrefetchScalarGridSpec(
        num_scalar_prefetch=3,
        grid=(nstep,),
        in_specs=[pl.BlockSpec(stblk, lambda g, s, f, l: (g, 0, 0, 0)),
                  pl.BlockSpec(stblk, lambda g, s, f, l: (nstep - 1 - g, 1, 0, 0)),
                  pl.BlockSpec((SCAN_CHUNKS, SUBLANES, SSD_WIDTH), lambda g, s, f, l: (g, 0, 0)),
                  pl.BlockSpec((SCAN_CHUNKS, SUBLANES, SSD_WIDTH), lambda g, s, f, l: (nstep - 1 - g, 0, 0)),
                  pl.BlockSpec(blk4, lambda g, s, f, l: (s[g], 0, 0, 0)),
                  pl.BlockSpec(blk4, lambda g, s, f, l: (s[nstep - 1 - g], 1, 0, 0))],
        out_specs=[pl.BlockSpec((SCAN_CHUNKS, SSD_STATE, SSD_WIDTH), lambda g, s, f, l: (g, 0, 0)),
                   pl.BlockSpec((SCAN_CHUNKS, SSD_STATE, SSD_WIDTH), lambda g, s, f, l: (nstep - 1 - g, 0, 0)),
                   pl.BlockSpec(blk4, lambda g, s, f, l: (s[g], 0, 0, 0)),
                   pl.BlockSpec(blk4, lambda g, s, f, l: (s[nstep - 1 - g], 0, 0, 0))],
        scratch_shapes=[pltpu.VMEM((SSD_STATE, SSD_WIDTH), jnp.float32),
                        pltpu.VMEM((SSD_STATE, SSD_WIDTH), jnp.float32)])
    return pl.pallas_call(
        _ssd_scan_kernel,
        out_shape=[jax.ShapeDtypeStruct((nch, SSD_STATE, SSD_WIDTH), jnp.float32),
                   jax.ShapeDtypeStruct((nch, SSD_STATE, SSD_WIDTH), jnp.float32),
                   jax.ShapeDtypeStruct((nseq, 1, SSD_STATE, SSD_WIDTH), jnp.float32),
                   jax.ShapeDtypeStruct((nseq, 1, SSD_STATE, SSD_WIDTH), jnp.float32)],
        grid_spec=grid_spec,
        compiler_params=_cparams(1),
        name="ssd_scan",
    )(seq_id, first, last, st, st, cdec, cdec, h0, h0)


def _outproj_kernel(x_ref, attn_ref, u_ref, y_ref, z_ref, cm_ref, din_ref, hsf_ref, hsb_ref, m_ref,
                    sw_ref, wo_ref, n2_ref, rwh_ref, rwl_ref, rb_ref, ef_ref, eb_ref,
                    x1_ref, xm_ref, ri_ref, rw_ref):
    for sub in range(TM_OUT // SUB_OUT):
        _outproj_subtile(sub, x_ref, attn_ref, u_ref, y_ref, z_ref, cm_ref, din_ref, hsf_ref, hsb_ref, m_ref,
                         sw_ref, wo_ref, n2_ref, rwh_ref, rwl_ref, rb_ref, ef_ref, eb_ref,
                         x1_ref, xm_ref, ri_ref, rw_ref)


def _outproj_subtile(sub, x_ref, attn_ref, u_ref, y_ref, z_ref, cm_ref, din_ref, hsf_ref, hsb_ref, m_ref,
                     sw_ref, wo_ref, n2_ref, rwh_ref, rwl_ref, rb_ref, ef_ref, eb_ref,
                     x1_ref, xm_ref, ri_ref, rw_ref):
    t = SSD_CHUNK
    hpg = SSD_HEADS // SSD_GROUPS
    gw = hpg * SSD_HEAD_DIM
    rows = slice(sub * SUB_OUT, (sub + 1) * SUB_OUT)
    ys = []
    for c in range(sub * (SUB_OUT // t), (sub + 1) * (SUB_OUT // t)):
        rs = slice(c * t, (c + 1) * t)
        cmc = cm_ref[rs, :]
        dinc = din_ref[rs, :]
        y = y_ref[rs, :]
        for hs_ref, e_ref in ((hsf_ref, ef_ref), (hsb_ref, eb_ref)):
            hs = hs_ref[c]
            yoff = jnp.concatenate(
                [_dot(_mx(cmc[:, gi * SSD_STATE:(gi + 1) * SSD_STATE]), _mx(hs[:, gi * gw:(gi + 1) * gw]))
                 for gi in range(SSD_GROUPS)], axis=1)
            y = y + yoff * _dot_sel2(dinc, e_ref[...])
        ys.append(y)
    y = jnp.concatenate(ys, axis=0) * _silu(z_ref[rows, :])
    nrm = []
    for gi in range(SSD_GROUPS):
        yg = y[:, gi * gw:(gi + 1) * gw]
        nrm.append(yg * lax.rsqrt(jnp.mean(yg * yg, axis=-1, keepdims=True) + EPS))
    y = jnp.concatenate(nrm, axis=1) * sw_ref[...]
    o1 = ATTN_WIDTH
    o2 = ATTN_WIDTH + SSD_WIDTH
    mix = (_dot(_mx(attn_ref[rows, :]), wo_ref[0:o1, :]) + _dot(_mx(y), wo_ref[o1:o2, :])
           + _dot(_mx(u_ref[rows, :]), wo_ref[o2:, :]))
    x1 = x_ref[rows, :] + m_ref[0, 2:3, :] * mix
    x1_ref[rows, :] = x1
    xm = (x1 * lax.rsqrt(jnp.mean(x1 * x1, axis=-1, keepdims=True) + EPS) * n2_ref[...]
          * (1.0 + m_ref[0, 4:5, :]) + m_ref[0, 3:4, :])
    _rows_to_tiles(xm_ref.at[rows], xm)
    xh, xl = _split2(xm)
    lg = _dot(xh, rwh_ref[...]) + _dot(xh, rwl_ref[...]) + _dot(xl, rwh_ref[...]) + rb_ref[...]
    lane_i = lax.broadcasted_iota(jnp.int32, lg.shape, 1)
    lane = lane_i.astype(jnp.float32)
    ninf = -jnp.inf
    big = float(LANES)
    is_g = lane_i < N_EXPERT_GROUPS
    gmax = jnp.max(jnp.where(is_g, lg, ninf), axis=-1, keepdims=True)
    gidx = jnp.min(jnp.where(is_g & (lg == gmax), lane, big), axis=-1, keepdims=True)
    p_grp = 1.0 / jnp.sum(jnp.where(is_g, jnp.exp(lg - gmax), 0.0), axis=-1, keepdims=True)
    lo = N_EXPERT_GROUPS + EXPERTS_PER_GROUP * gidx
    in_e = (lane >= lo) & (lane < lo + EXPERTS_PER_GROUP)
    v1 = jnp.max(jnp.where(in_e, lg, ninf), axis=-1, keepdims=True)
    i1 = jnp.min(jnp.where(in_e & (lg == v1), lane, big), axis=-1, keepdims=True)
    rest = in_e & (lane != i1)
    v2 = jnp.max(jnp.where(rest, lg, ninf), axis=-1, keepdims=True)
    i2 = jnp.min(jnp.where(rest & (lg == v2), lane, big), axis=-1, keepdims=True)
    e2 = jnp.exp(v2 - v1)
    w1 = p_grp * (1.0 / (1.0 + e2))
    w2 = p_grp * (e2 / (1.0 + e2))
    e_first = (i1 - N_EXPERT_GROUPS).astype(jnp.int32)
    e_second = (i2 - N_EXPERT_GROUPS).astype(jnp.int32)
    ri_ref[rows, :] = jnp.where(lane_i == 0, e_first, jnp.where(lane_i == 1, e_second, 0))
    rw_ref[rows, :] = jnp.where(lane_i == 0, w1, jnp.where(lane_i == 1, w2, 0.0))


def _out_projection(x, attn, u, ydiag, z, cm, din, hsf, hsb, mod_l, pr, consts, *, n_prompt, sample_len):
    n = x.shape[0]
    tm = TM_OUT
    cpt = tm // SSD_CHUNK

    def cond_row(t):
        s = t * tm
        return jnp.where(s < n_prompt, 0, 1 + (s - n_prompt) // sample_len)

    def rows(width):
        return pl.BlockSpec((tm, width), lambda t: (t, 0))

    params = [pr["ssd_norm_w"], pr["w_out"], pr["norm2"], pr["rw_hi"], pr["rw_lo"], pr["rb"],
              consts["ef"], consts["eb"]]
    return pl.pallas_call(
        _outproj_kernel,
        out_shape=[jax.ShapeDtypeStruct((n, D_MODEL), jnp.float32),
                   jax.ShapeDtypeStruct((n, ROW_TILES, LANES), jnp.float32),
                   jax.ShapeDtypeStruct((n, LANES), jnp.int32),
                   jax.ShapeDtypeStruct((n, LANES), jnp.float32)],
        grid=(n // tm,),
        in_specs=[rows(D_MODEL), rows(ATTN_WIDTH), rows(CONV_WIDTH), rows(SSD_WIDTH), rows(SSD_WIDTH),
                  rows(LANES), rows(LANES),
                  pl.BlockSpec((cpt, SSD_STATE, SSD_WIDTH), lambda t: (t, 0, 0)),
                  pl.BlockSpec((cpt, SSD_STATE, SSD_WIDTH), lambda t: (t, 0, 0)),
                  pl.BlockSpec((1, 6, D_MODEL), lambda t: (cond_row(t), 0, 0))]
        + [_full(p.shape) for p in params],
        out_specs=[rows(D_MODEL), pl.BlockSpec((tm, ROW_TILES, LANES), lambda t: (t, 0, 0)),
                   rows(LANES), rows(LANES)],
        compiler_params=_cparams(1),
        name="out_projection",
    )(x, attn, u, ydiag, z, cm, din, hsf, hsb, mod_l, *params)


def _route_slots(ridx):
    n = ridx.shape[0]
    n_assign = 2 * n
    blk = MOE_BLK
    flat = ridx[:, :2].reshape(-1)
    onehot = (flat[:, None] == jnp.arange(N_EXPERTS, dtype=jnp.int32)[None, :]).astype(jnp.int32)
    csum = jnp.cumsum(onehot, axis=0)
    counts = csum[-1]
    padded = (counts + blk - 1) // blk * blk
    pad_end = jnp.cumsum(padded)
    pad_start = pad_end - padded
    slot = jnp.sum(onehot * (csum - 1 + pad_start[None, :]), axis=1).astype(jnp.int32)
    n_blocks = n_assign // blk + N_EXPERTS
    src = jnp.zeros((n_blocks * blk,), jnp.int32).at[slot].set(jnp.arange(n_assign, dtype=jnp.int32) // 2)
    starts = jnp.arange(n_blocks, dtype=jnp.int32) * blk
    block_e = jnp.minimum(jnp.sum((pad_end[None, :] <= starts[:, None]).astype(jnp.int32), axis=1),
                          N_EXPERTS - 1).astype(jnp.int32)
    n_valid = (pad_end[-1] // blk).astype(jnp.int32).reshape(1)
    return slot, src, block_e, n_valid


def _gather_rows(idx_ref, base, src_hbm, dst, sem, n_rows):
    def body(j, carry):
        for u in range(SUBLANES):
            r = idx_ref[base + j * SUBLANES + u]
            pltpu.make_async_copy(src_hbm.at[r], dst.at[j, :, u, :], sem).start()
        return carry

    lax.fori_loop(0, n_rows // SUBLANES, body, 0)


def _wait_rows(dst, sem):
    pltpu.make_async_copy(dst, dst, sem).wait()


def _rows_from_groups(ref, g0, g1):
    return jnp.concatenate(
        [jnp.concatenate([ref[g, s] for s in range(ROW_TILES)], axis=1) for g in range(g0, g1)], axis=0)


def _experts_kernel(be_ref, src_ref, nv_ref, x_hbm, wg_ref, wu_ref, wd_ref, o_ref,
                    xbuf, sem, wg_c, wu_c, wd_c):
    b = pl.program_id(0)
    nv = nv_ref[0]
    blk = MOE_BLK

    @pl.when(b == 0)
    def _():
        _gather_rows(src_ref, 0, x_hbm, xbuf.at[0], sem.at[0], blk)

    @pl.when(b + 1 < nv)
    def _():
        nxt = (b + 1) % 2
        _gather_rows(src_ref, (b + 1) * blk, x_hbm, xbuf.at[nxt], sem.at[nxt], blk)

    @pl.when((b == 0) | (be_ref[b] != be_ref[jnp.maximum(b - 1, 0)]))
    def _():
        wg_c[...] = _mx(wg_ref[0, 0])
        wu_c[...] = _mx(wu_ref[0, 0])
        wd_c[...] = _mx(wd_ref[0, 0])

    @pl.when(b < nv)
    def _():
        cur = b % 2
        _wait_rows(xbuf.at[cur], sem.at[cur])
        xb = _mx(_rows_from_groups(xbuf.at[cur], 0, blk // SUBLANES))
        hid = _silu(_dot(xb, wg_c[...])) * _dot(xb, wu_c[...])
        _rows_to_tiles(o_ref, _dot(_mx(hid), wd_c[...]))

    @pl.when(b >= nv)
    def _():
        o_ref[...] = jnp.zeros_like(o_ref)


def _experts(block_e, src, n_valid, xm, wg, wu, wd, layer):
    blk = MOE_BLK
    n_blocks = block_e.shape[0]
    grid_spec = pltpu.PrefetchScalarGridSpec(
        num_scalar_prefetch=3,
        grid=(n_blocks,),
        in_specs=[pl.BlockSpec(memory_space=pl.ANY),
                  pl.BlockSpec((1, 1, D_MODEL, EXPERT_FF), lambda b, be, s, nv: (layer, be[b], 0, 0)),
                  pl.BlockSpec((1, 1, D_MODEL, EXPERT_FF), lambda b, be, s, nv: (layer, be[b], 0, 0)),
                  pl.BlockSpec((1, 1, EXPERT_FF, D_MODEL), lambda b, be, s, nv: (layer, be[b], 0, 0))],
        out_specs=pl.BlockSpec((blk, ROW_TILES, LANES), lambda b, be, s, nv: (b, 0, 0)),
        scratch_shapes=[pltpu.VMEM((2, blk // SUBLANES, ROW_TILES, SUBLANES, LANES), jnp.float32),
                        pltpu.SemaphoreType.DMA((2,)),
                        pltpu.VMEM((D_MODEL, EXPERT_FF), MXU_DTYPE),
                        pltpu.VMEM((D_MODEL, EXPERT_FF), MXU_DTYPE),
                        pltpu.VMEM((EXPERT_FF, D_MODEL), MXU_DTYPE)])
    return pl.pallas_call(
        _experts_kernel,
        out_shape=jax.ShapeDtypeStruct((n_blocks * blk, ROW_TILES, LANES), jnp.float32),
        grid_spec=grid_spec,
        compiler_params=_cparams(1),
        name="experts",
    )(block_e, src, n_valid, xm, wg, wu, wd)


def _combine_kernel(slot_ref, x1_ref, m_ref, rw_ref, ys_hbm, o_ref, ybuf, sem):
    i = pl.program_id(0)
    n_steps = pl.num_programs(0)
    rows = 2 * TM_COMB

    @pl.when(i == 0)
    def _():
        _gather_rows(slot_ref, 0, ys_hbm, ybuf.at[0], sem.at[0], rows)

    @pl.when(i + 1 < n_steps)
    def _():
        nxt = (i + 1) % 2
        _gather_rows(slot_ref, (i + 1) * rows, ys_hbm, ybuf.at[nxt], sem.at[nxt], rows)

    cur = i % 2
    _wait_rows(ybuf.at[cur], sem.at[cur])
    rw = rw_ref[...]
    groups = TM_COMB // SUBLANES
    ya = _rows_from_groups(ybuf.at[cur], 0, groups)
    yb = _rows_from_groups(ybuf.at[cur], groups, 2 * groups)
    o_ref[...] = x1_ref[...] + m_ref[0, 5:6, :] * (rw[:, 0:1] * ya + rw[:, 1:2] * yb)


def _combine(slot_km, x1, mod_l, rw, ys, *, n_prompt, sample_len):
    n = x1.shape[0]
    tm = TM_COMB

    def cond_row(t, *_):
        s = t * tm
        return jnp.where(s < n_prompt, 0, 1 + (s - n_prompt) // sample_len)

    grid_spec = pltpu.PrefetchScalarGridSpec(
        num_scalar_prefetch=1,
        grid=(n // tm,),
        in_specs=[pl.BlockSpec((tm, D_MODEL), lambda t, s: (t, 0)),
                  pl.BlockSpec((1, 6, D_MODEL), lambda t, s: (cond_row(t), 0, 0)),
                  pl.BlockSpec((tm, LANES), lambda t, s: (t, 0)),
                  pl.BlockSpec(memory_space=pl.ANY)],
        out_specs=pl.BlockSpec((tm, D_MODEL), lambda t, s: (t, 0)),
        scratch_shapes=[pltpu.VMEM((2, 2 * tm // SUBLANES, ROW_TILES, SUBLANES, LANES), jnp.float32),
                        pltpu.SemaphoreType.DMA((2,))])
    return pl.pallas_call(
        _combine_kernel,
        out_shape=jax.ShapeDtypeStruct((n, D_MODEL), jnp.float32),
        grid_spec=grid_spec,
        compiler_params=_cparams(1),
        name="combine",
    )(slot_km, x1, mod_l, rw, ys)


def _pad_lanes(v, width=LANES):
    return jnp.pad(v, [(0, 0)] * (v.ndim - 1) + [(0, width - v.shape[-1])])


def _constants(sample_len):
    r = jnp.arange(LANES)
    c512 = jnp.arange(SSD_WIDTH)
    ef = (r[:, None] == (c512[None, :] // SSD_HEAD_DIM)).astype(MXU_DTYPE)
    eb = (r[:, None] == (c512[None, :] // SSD_HEAD_DIM + SSD_HEADS)).astype(MXU_DTYPE)
    t = jnp.arange(SSD_CHUNK)
    tril = (t[None, :] <= t[:, None]).astype(MXU_DTYPE)
    triu = (t[None, :] >= t[:, None]).astype(MXU_DTYPE)
    qk = jnp.arange(COL_V)
    bd_qk = (qk[:, None] // HEAD_DIM == qk[None, :] // HEAD_DIM).astype(MXU_DTYPE)
    cw = jnp.arange(CONV_WIDTH)
    gsz = CONV_WIDTH // CONV_GROUPS
    bd_cm = (cw[:, None] // gsz == cw[None, :] // gsz).astype(MXU_DTYPE)
    pos = jnp.arange(sample_len)
    row = (pos // GRID_W).astype(jnp.float32)
    col = (pos % GRID_W).astype(jnp.float32)
    n_freq = HEAD_DIM // 4
    inv = ROPE_THETA ** (-jnp.arange(n_freq, dtype=jnp.float32) / n_freq)
    ar = row[:, None] * inv
    ac = col[:, None] * inv
    cos64 = jnp.concatenate([jnp.cos(ar), jnp.cos(ar), jnp.cos(ac), jnp.cos(ac)], axis=1)
    sin64 = jnp.concatenate([-jnp.sin(ar), jnp.sin(ar), -jnp.sin(ac), jnp.sin(ac)], axis=1)
    cos_t = jnp.concatenate([jnp.tile(cos64, (1, 2)), jnp.ones((TM_PROJ, LANES), jnp.float32)], axis=0)
    sin_t = jnp.concatenate([jnp.tile(sin64, (1, 2)), jnp.zeros((TM_PROJ, LANES), jnp.float32)], axis=0)
    return dict(ef=ef, eb=eb, tril=tril, triu=triu, bd_qk=bd_qk, bd_cm=bd_cm, cos=cos_t, sin=sin_t)


def _pack_w_in(w_in):
    o = 0
    parts = {}
    for name, size in (("q", ATTN_WIDTH), ("k", KV_WIDTH), ("v", KV_WIDTH), ("z", SSD_WIDTH),
                       ("xbc", SSD_XBC), ("dt", SSD_HEADS), ("glu", 2 * CONV_WIDTH)):
        parts[name] = w_in[:, o:o + size]
        o += size
    dt2 = _pad_lanes(jnp.concatenate([parts["dt"], parts["dt"]], axis=1))
    return _mx(jnp.concatenate([parts["q"], parts["k"], parts["v"], parts["z"], parts["xbc"],
                                parts["glu"], dt2], axis=1))


def _layer_params(l, norm1_w, norm2_w, w_in, q_norm_w, k_norm_w, attn_sink, ssd_conv_w, ssd_conv_b,
                  ssd_dt_bias, ssd_a_log, ssd_d, ssd_norm_w, cm_conv_w, cm_conv_b, cm_norm_w, cm_norm_b,
                  w_out, router_group_w, router_group_b, router_expert_w, router_expert_b):
    rw = _pad_lanes(jnp.concatenate([router_group_w[l], router_expert_w[l]], axis=1))
    rw_hi = _mx(rw)
    rw_lo = _mx(rw - _f32(rw_hi))
    return dict(
        norm1=norm1_w[l][None, :],
        norm2=norm2_w[l][None, :],
        w_in=_pack_w_in(w_in[l]),
        qkw=jnp.concatenate([jnp.tile(q_norm_w[l], ATTN_HEADS), jnp.tile(k_norm_w[l], KV_HEADS)])[None, :],
        sink=attn_sink[l],
        ssd_conv_w=ssd_conv_w[l],
        ssd_conv_b=ssd_conv_b[l][None, :],
        dtb=_pad_lanes(ssd_dt_bias[l].reshape(1, 2 * SSD_HEADS)),
        alog=_pad_lanes(ssd_a_log[l].reshape(1, 2 * SSD_HEADS)),
        dsum=jnp.repeat(ssd_d[l, 0] + ssd_d[l, 1], SSD_HEAD_DIM)[None, :],
        ssd_norm_w=ssd_norm_w[l][None, :],
        cm_conv_w=cm_conv_w[l],
        cm_conv_b=cm_conv_b[l][None, :],
        cm_norm_w=cm_norm_w[l][None, :],
        cm_norm_b=cm_norm_b[l][None, :],
        w_out=_mx(w_out[l]),
        rw_hi=rw_hi,
        rw_lo=rw_lo,
        rb=_pad_lanes(jnp.concatenate([router_group_b[l], router_expert_b[l]])[None, :]),
    )


def _chunk_tables(n_prompt_seq, prompt_len, n_sample_seq, sample_len):
    t = SSD_CHUNK * SCAN_CHUNKS
    assert prompt_len % t == 0 and sample_len % t == 0
    pc, sc = prompt_len // t, sample_len // t
    seq = [s for s in range(n_prompt_seq) for _ in range(pc)]
    seq += [n_prompt_seq + s for s in range(n_sample_seq) for _ in range(sc)]
    first = [int(i == 0) for _ in range(n_prompt_seq) for i in range(pc)]
    first += [int(i == 0) for _ in range(n_sample_seq) for i in range(sc)]
    last = [int(i == pc - 1) for _ in range(n_prompt_seq) for i in range(pc)]
    last += [int(i == sc - 1) for _ in range(n_sample_seq) for i in range(sc)]
    return (jnp.asarray(seq, jnp.int32), jnp.asarray(first, jnp.int32), jnp.asarray(last, jnp.int32))


def _layer(x, mod_l, pr, consts, tables, ck, cv, h0, wg, wu, wd, layer, *, n_prompt, prompt_len, sample_len):
    n = x.shape[0]
    n_sample = n - n_prompt
    q, kr, kn, v, z, xbc, glu, dt = _in_projection(
        x, mod_l, pr["norm1"], pr["w_in"], pr["qkw"], consts["bd_qk"], consts["cos"], consts["sin"],
        n_prompt, sample_len)
    attn_p = _attention(pr["sink"], q, kr, v, (kr, 0), (v, 0), tok_off=0, n_tok=n_prompt,
                        seq_len=prompt_len, ctx_len=prompt_len, local=False)
    attn_s = _attention(pr["sink"], q, kr, v, (ck, 0), (cv, 0), tok_off=n_prompt, n_tok=n_sample,
                        seq_len=sample_len, ctx_len=ck.shape[0] // (n_sample // sample_len), local=True)
    attn = jnp.concatenate([attn_p, attn_s], axis=0)
    ydiag, st, cdec, cm, din, u = _ssd_local(xbc, glu, dt, pr, consts, n_prompt=n_prompt,
                                             prompt_len=prompt_len, sample_len=sample_len)
    hsf, hsb, hff, hfb = _ssd_scan(*tables, st, cdec, h0)
    x1, xm, ridx, rwt = _out_projection(x, attn, u, ydiag, z, cm, din, hsf, hsb, mod_l, pr, consts,
                                        n_prompt=n_prompt, sample_len=sample_len)
    slot, src, block_e, n_valid = _route_slots(ridx)
    ys = _experts(block_e, src, n_valid, xm, wg, wu, wd, layer)
    slot_km = slot.reshape(n // TM_COMB, TM_COMB, 2).transpose(0, 2, 1).reshape(-1)
    x2 = _combine(slot_km, x1, mod_l, rwt, ys, n_prompt=n_prompt, sample_len=sample_len)
    return x2, kn, v, hff, hfb


def kernel(x_prompt, x_sample, c, cache_k, cache_v, state_ssm, c_ctx, mod_w, mod_b, norm1_w, norm2_w, w_in, q_norm_w, k_norm_w, attn_sink, ssd_conv_w, ssd_conv_b, ssd_dt_bias, ssd_a_log, ssd_d, ssd_norm_w, cm_conv_w, cm_conv_b, cm_norm_w, cm_norm_b, w_out, router_group_w, router_group_b, router_expert_w, router_expert_b, expert_w_gate, expert_w_up, expert_w_down):
    bp, lp, d = x_prompt.shape
    bs, ls, _ = x_sample.shape
    depth = mod_w.shape[0]
    past = cache_k.shape[2]
    n_prompt = bp * lp
    assert d == D_MODEL and bs + 1 <= SUBLANES
    assert all(v % tm == 0 for v in (n_prompt, ls) for tm in (TM_OUT, TM_PROJ, TM_COMB))
    assert lp % SSD_CHUNK == 0 and ls % SSD_CHUNK == 0

    cond8 = jnp.concatenate([c_ctx[None, :], c, jnp.zeros((SUBLANES - 1 - bs, d), jnp.float32)], axis=0)
    mod = _modulation(cond8, mod_w, mod_b).reshape(depth, SUBLANES, 6, d)
    consts = _constants(ls)
    tables = _chunk_tables(bp, lp, bs, ls)

    x = jnp.concatenate([x_prompt.reshape(n_prompt, d), x_sample.reshape(bs * ls, d)], axis=0)
    ks, vs, sts = [], [], []
    for l in range(depth):
        pr = _layer_params(l, norm1_w, norm2_w, w_in, q_norm_w, k_norm_w, attn_sink, ssd_conv_w, ssd_conv_b,
                           ssd_dt_bias, ssd_a_log, ssd_d, ssd_norm_w, cm_conv_w, cm_conv_b, cm_norm_w,
                           cm_norm_b, w_out, router_group_w, router_group_b, router_expert_w,
                           router_expert_b)
        ck = cache_k[:, l].reshape(bs * past, KV_WIDTH)
        cv = cache_v[:, l].reshape(bs * past, KV_WIDTH)
        h0s = state_ssm[:, l].transpose(0, 1, 4, 2, 3).reshape(bs, 2, SSD_STATE, SSD_WIDTH)
        h0 = jnp.concatenate([jnp.zeros((bp, 2, SSD_STATE, SSD_WIDTH), jnp.float32), h0s], axis=0)
        x, kn, v, hff, hfb = _layer(x, mod[l], pr, consts, tables, ck, cv, h0,
                                    expert_w_gate, expert_w_up, expert_w_down, l,
                                    n_prompt=n_prompt, prompt_len=lp, sample_len=ls)
        ks.append(kn[:n_prompt].reshape(bp, lp, KV_HEADS, HEAD_DIM))
        vs.append(v[:n_prompt].reshape(bp, lp, KV_HEADS, HEAD_DIM))
        hfin = jnp.concatenate([hff[:bp], hfb[:bp]], axis=1)
        sts.append(hfin.reshape(bp, 2, SSD_STATE, SSD_HEADS, SSD_HEAD_DIM).transpose(0, 1, 3, 4, 2))
    y_prompt = x[:n_prompt].reshape(bp, lp, d)
    y_sample = x[n_prompt:].reshape(bs, ls, d)
    return (y_prompt, y_sample, jnp.stack(ks, axis=1), jnp.stack(vs, axis=1), jnp.stack(sts, axis=1))
```
